```python
import jax, jax.numpy as jnp
from jax import lax
import numpy as np

D_MODEL = 4096
BATCH = 2
SEQ = 8192
DEPTH = 4

N_A = DEPTH // 2
N_B = DEPTH - N_A
D_FF = 4 * D_MODEL
CONV_K = 31
N_HEADS = 32
HEAD_DIM = D_MODEL // N_HEADS
N_KV_GROUPS = 4
HEADS_PER_GROUP = N_HEADS // N_KV_GROUPS
N_BRANCH = 3
CMP_BLK = 32
CMP_STRIDE = 16
SEL_BLK = 64
N_SEL = 16
WINDOW = 512
CMP_HID = 256
Q_BLK = 128
N_MOD = 6
EPS = 1e-6
NEG = -1e30
BIG = 1e9
TINY = 1e-20

kernel_name = "yoco_conformer_nsa_hybrid"


def rms_norm(x, g):
    xf = x.astype(jnp.float32)
    y = xf * lax.rsqrt(jnp.mean(xf * xf, axis=-1, keepdims=True) + EPS)
    return (y * g.astype(jnp.float32)).astype(x.dtype)


def layer_norm(x, g, b):
    xf = x.astype(jnp.float32)
    xc = xf - jnp.mean(xf, axis=-1, keepdims=True)
    y = xc * lax.rsqrt(jnp.mean(xc * xc, axis=-1, keepdims=True) + EPS)
    return (y * g.astype(jnp.float32) + b.astype(jnp.float32)).astype(x.dtype)


def modulate(h, shift, scale):
    return h * (1 + scale[:, None, :]) + shift[:, None, :]


def masked_softmax(s, mask):
    s = jnp.where(mask, s, NEG)
    m = jnp.max(s, axis=-1, keepdims=True)
    p = jnp.exp(s - m) * mask
    return p / jnp.maximum(jnp.sum(p, axis=-1, keepdims=True), TINY)


def conformer_conv(h, w_pw1, b_pw1, w_dw, b_dw, ln_g, ln_b, w_pw2, b_pw2):
    u = h @ w_pw1 + b_pw1
    a, gt = jnp.split(u, 2, axis=-1)
    u = a * jax.nn.sigmoid(gt)
    u = lax.conv_general_dilated(u, w_dw[:, None, :], window_strides=(1,), padding=[(CONV_K - 1, 0)],
                                 dimension_numbers=("NWC", "WIO", "NWC"),
                                 feature_group_count=u.shape[-1]) + b_dw
    u = jax.nn.silu(layer_norm(u, ln_g, ln_b))
    return u @ w_pw2 + b_pw2


def sq_relu_mlp(h, w1, w2):
    return jnp.square(jax.nn.relu(h @ w1)) @ w2


def cmp_to_sel_weights(nc, ns):
    cs = jnp.arange(nc) * CMP_STRIDE
    ce = cs + CMP_BLK
    ss = jnp.arange(ns) * SEL_BLK
    se = ss + SEL_BLK
    ov = jnp.clip(jnp.minimum(ce[:, None], se[None, :]) - jnp.maximum(cs[:, None], ss[None, :]), 0, None)
    return (ov / CMP_STRIDE).astype(jnp.float32)


def shared_kv(x, kv_norm_g, w_kv, k_norm_g, cmp_pos, cmp_w1, cmp_b1, cmp_w2, cmp_b2):
    B, S, _ = x.shape
    G, dh = N_KV_GROUPS, HEAD_DIM
    h = rms_norm(x, kv_norm_g)
    kv = (h @ w_kv).reshape(B, S, 2 * N_BRANCH, G, dh).transpose(2, 0, 3, 1, 4)
    kc, vc, ks, vs, kw, vw = kv[0], kv[1], kv[2], kv[3], kv[4], kv[5]
    nc = (S - CMP_BLK) // CMP_STRIDE + 1
    idx = jnp.arange(nc)[:, None] * CMP_STRIDE + jnp.arange(CMP_BLK)[None, :]

    def compress(t, i):
        blocks = t[:, :, idx] + cmp_pos[i]
        flat = blocks.reshape(B, G, nc, CMP_BLK * dh)
        return jax.nn.silu(flat @ cmp_w1[i] + cmp_b1[i]) @ cmp_w2[i] + cmp_b2[i]

    kc = rms_norm(compress(kc, 0), k_norm_g[0])
    vc = compress(vc, 1)
    ks = rms_norm(ks, k_norm_g[1]).reshape(B, G, S // SEL_BLK, SEL_BLK, dh)
    vs = vs.reshape(B, G, S // SEL_BLK, SEL_BLK, dh)
    kw = rms_norm(kw, k_norm_g[2])
    return kc, vc, ks, vs, kw, vw


def nsa_attention(h, w_qg, q_norm_g, w_o, kc, vc, ksb, vsb, kw, vw):
    B, S, _ = h.shape
    G, R, dh = N_KV_GROUPS, HEADS_PER_GROUP, HEAD_DIM
    HD = N_HEADS * HEAD_DIM
    qg = h @ w_qg
    q = rms_norm(qg[..., :HD].reshape(B, S, G, R, dh), q_norm_g)
    gates = jax.nn.sigmoid(qg[..., HD:].astype(jnp.float32)).reshape(B, S, G, R, N_BRANCH)
    nc = kc.shape[2]
    ns = ksb.shape[2]
    n_sel = min(N_SEL, ns)
    w_sel = cmp_to_sel_weights(nc, ns)
    kw_pad = jnp.pad(kw, ((0, 0), (0, 0), (WINDOW, 0), (0, 0)))
    vw_pad = jnp.pad(vw, ((0, 0), (0, 0), (WINDOW, 0), (0, 0)))
    scale = dh ** -0.5
    bi = jnp.arange(B)[:, None, None, None]
    gi = jnp.arange(G)[None, :, None, None]
    cmp_end = jnp.arange(nc) * CMP_STRIDE + CMP_BLK - 1
    jblk = jnp.arange(ns)

    def block(q0):
        t_pos = q0 + jnp.arange(Q_BLK)
        qb = lax.dynamic_slice_in_dim(q, q0, Q_BLK, axis=1)
        gb = lax.dynamic_slice_in_dim(gates, q0, Q_BLK, axis=1)
        s_c = jnp.einsum("btgrd,bgnd->bgrtn", qb, kc).astype(jnp.float32) * scale
        p_c = masked_softmax(s_c, cmp_end[None, :] <= t_pos[:, None])
        o_c = jnp.einsum("bgrtn,bgnd->btgrd", p_c.astype(vc.dtype), vc)
        imp = jnp.einsum("bgrtn,nj->bgtj", p_c, w_sel)
        cur = t_pos // SEL_BLK
        forced = (jblk[None, :] == 0) | (jblk[None, :] == cur[:, None]) | (jblk[None, :] == cur[:, None] - 1)
        score = jnp.where(forced, BIG, imp)
        score = jnp.where(jblk[None, :] <= cur[:, None], score, -BIG)
        _, idx = lax.top_k(score, n_sel)
        k_sel = ksb[bi, gi, idx].reshape(B, G, Q_BLK, n_sel * SEL_BLK, dh)
        v_sel = vsb[bi, gi, idx].reshape(B, G, Q_BLK, n_sel * SEL_BLK, dh)
        kpos = (idx[..., None] * SEL_BLK + jnp.arange(SEL_BLK)).reshape(B, G, Q_BLK, n_sel * SEL_BLK)
        mask_s = (kpos <= t_pos[None, None, :, None])[:, :, None]
        s_s = jnp.einsum("btgrd,bgtmd->bgrtm", qb, k_sel).astype(jnp.float32) * scale
        p_s = masked_softmax(s_s, mask_s)
        o_s = jnp.einsum("bgrtm,bgtmd->btgrd", p_s.astype(v_sel.dtype), v_sel)
        k_win = lax.dynamic_slice_in_dim(kw_pad, q0, WINDOW + Q_BLK, axis=2)
        v_win = lax.dynamic_slice_in_dim(vw_pad, q0, WINDOW + Q_BLK, axis=2)
        kp = q0 - WINDOW + jnp.arange(WINDOW + Q_BLK)
        diff = t_pos[:, None] - kp[None, :]
        mask_w = (diff >= 0) & (diff < WINDOW) & (kp[None, :] >= 0)
        s_w = jnp.einsum("btgrd,bgmd->bgrtm", qb, k_win).astype(jnp.float32) * scale
        p_w = masked_softmax(s_w, mask_w)
        o_w = jnp.einsum("bgrtm,bgmd->btgrd", p_w.astype(v_win.dtype), v_win)
        o = gb[..., 0:1] * o_c + gb[..., 1:2] * o_s + gb[..., 2:3] * o_w
        return o.reshape(B, Q_BLK, HD).astype(h.dtype)

    outs = lax.map(block, jnp.arange(S // Q_BLK) * Q_BLK)
    o = outs.transpose(1, 0, 2, 3).reshape(B, S, HD)
    return o @ w_o


def setup_inputs(seed: int = 0) -> dict:
    key = jax.random.key(seed)
    ks = jax.random.split(key, 32)
    f32 = jnp.float32
    HD = N_HEADS * HEAD_DIM

    def nrm(k, shape, s):
        return jax.random.normal(k, shape, f32) * s

    def gain(k, shape):
        return 1.0 + nrm(k, shape, 0.02)

    return {
        "x": nrm(ks[0], (BATCH, SEQ, D_MODEL), 1.0),
        "c": nrm(ks[1], (BATCH, D_MODEL), 1.0),
        "w_ada": nrm(ks[2], (D_MODEL, N_MOD * D_MODEL), 0.5 * D_MODEL ** -0.5),
        "b_ada": nrm(ks[3], (N_MOD * D_MODEL,), 0.01),
        "ada_table": nrm(ks[4], (DEPTH, N_MOD, D_MODEL), 0.1),
        "norm_mix_g": gain(ks[5], (DEPTH, D_MODEL)),
        "norm_ffn_g": gain(ks[6], (DEPTH, D_MODEL)),
        "w_ffn1": nrm(ks[7], (DEPTH, D_MODEL, D_FF), D_MODEL ** -0.5),
        "w_ffn2": nrm(ks[8], (DEPTH, D_FF, D_MODEL), D_FF ** -0.5),
        "conv_w_pw1": nrm(ks[9], (N_A, D_MODEL, 2 * D_MODEL), D_MODEL ** -0.5),
        "conv_b_pw1": nrm(ks[10], (N_A, 2 * D_MODEL), 0.01),
        "conv_w_dw": nrm(ks[11], (N_A, CONV_K, D_MODEL), CONV_K ** -0.5),
        "conv_b_dw": nrm(ks[12], (N_A, D_MODEL), 0.01),
        "conv_ln_g": gain(ks[13], (N_A, D_MODEL)),
        "conv_ln_b": nrm(ks[14], (N_A, D_MODEL), 0.01),
        "conv_w_pw2": nrm(ks[15], (N_A, D_MODEL, D_MODEL), D_MODEL ** -0.5),
        "conv_b_pw2": nrm(ks[16], (N_A, D_MODEL), 0.01),
        "kv_norm_g": gain(ks[17], (D_MODEL,)),
        "w_kv": nrm(ks[18], (D_MODEL, 2 * N_BRANCH * N_KV_GROUPS * HEAD_DIM), D_MODEL ** -0.5),
        "k_norm_g": gain(ks[19], (N_BRANCH, HEAD_DIM)),
        "cmp_pos": nrm(ks[20], (2, CMP_BLK, HEAD_DIM), 0.1),
        "cmp_w1": nrm(ks[21], (2, CMP_BLK * HEAD_DIM, CMP_HID), (CMP_BLK * HEAD_DIM) ** -0.5),
        "cmp_b1": nrm(ks[22], (2, CMP_HID), 0.01),
        "cmp_w2": nrm(ks[23], (2, CMP_HID, HEAD_DIM), CMP_HID ** -0.5),
        "cmp_b2": nrm(ks[24], (2, HEAD_DIM), 0.01),
        "attn_w_qg": nrm(ks[25], (N_B, D_MODEL, HD + N_BRANCH * N_HEADS), D_MODEL ** -0.5),
        "attn_q_norm_g": gain(ks[26], (N_B, HEAD_DIM)),
        "attn_w_o": nrm(ks[27], (N_B, HD, D_MODEL), HD ** -0.5),
    }


def reference(x, c, w_ada, b_ada, ada_table, norm_mix_g, norm_ffn_g, w_ffn1, w_ffn2,
              conv_w_pw1, conv_b_pw1, conv_w_dw, conv_b_dw, conv_ln_g, conv_ln_b, conv_w_pw2, conv_b_pw2,
              kv_norm_g, w_kv, k_norm_g, cmp_pos, cmp_w1, cmp_b1, cmp_w2, cmp_b2,
              attn_w_qg, attn_q_norm_g, attn_w_o):
    B = x.shape[0]
    mod_all = (jax.nn.silu(c) @ w_ada + b_ada).reshape(B, N_MOD, D_MODEL)
    kv = None
    for l in range(DEPTH):
        mod = mod_all + ada_table[l]
        sh_m, sc_m, g_m = mod[:, 0], mod[:, 1], mod[:, 2]
        sh_f, sc_f, g_f = mod[:, 3], mod[:, 4], mod[:, 5]
        h = modulate(rms_norm(x, norm_mix_g[l]), sh_m, sc_m)
        if l < N_A:
            y = conformer_conv(h, conv_w_pw1[l], conv_b_pw1[l], conv_w_dw[l], conv_b_dw[l],
                               conv_ln_g[l], conv_ln_b[l], conv_w_pw2[l], conv_b_pw2[l])
        else:
            i = l - N_A
            y = nsa_attention(h, attn_w_qg[i], attn_q_norm_g[i], attn_w_o[i], *kv)
        x = x + g_m[:, None, :] * y
        h = modulate(rms_norm(x, norm_ffn_g[l]), sh_f, sc_f)
        x = x + g_f[:, None, :] * sq_relu_mlp(h, w_ffn1[l], w_ffn2[l])
        if l == N_A - 1:
            kv = shared_kv(x, kv_norm_g, w_kv, k_norm_g, cmp_pos, cmp_w1, cmp_b1, cmp_w2, cmp_b2)
    return x
```

```python
import functools

import jax
import jax.numpy as jnp
from jax import lax
from jax.experimental import pallas as pl
from jax.experimental.pallas import tpu as pltpu

F32 = jnp.float32
BF16 = jnp.bfloat16

CMP_STRIDE = 16
SEL_BLK = 64
N_SEL = 16
WINDOW = 512
N_BRANCH = 3
N_MOD = 6
EPS = 1e-6
NEG = -1e30
BIG = 1e9
TINY = 1e-20

LANES = 128
SUBLANES = 8
VMEM_LIMIT_BYTES = 56 * 1024 * 1024

ATT_Q_TILE = 128
ATT_KV_TILE = 512
CONV_HALO = 32


def _tile(dim, pref):
    t = min(dim, pref)
    assert dim % t == 0, (dim, pref)
    return t


def _params(*sem):
    return pltpu.CompilerParams(dimension_semantics=sem, vmem_limit_bytes=VMEM_LIMIT_BYTES)


def _dot(a, b):
    return jnp.dot(a, b, preferred_element_type=F32)


def _dot_nt(a, b):
    return lax.dot_general(a, b, (((1,), (1,)), ((), ())), preferred_element_type=F32)


def _sigmoid(x):
    return 1.0 / (1.0 + jnp.exp(-x))


def _ada_kernel(c_ref, w_ref, b_ref, tab_ref, o_ref):
    c = c_ref[...]
    a = (c * _sigmoid(c)).astype(BF16)
    base = _dot(a, w_ref[...].astype(BF16)) + b_ref[...]
    for l in range(o_ref.shape[0]):
        o_ref[l] = base + tab_ref[l]


def _ada_mod(c, w_ada, b_ada, ada_table):
    B, D = c.shape
    depth = ada_table.shape[0]
    N = w_ada.shape[1]
    bn = _tile(N, 1024)
    return pl.pallas_call(
        _ada_kernel,
        grid=(N // bn,),
        in_specs=[
            pl.BlockSpec((B, D), lambda j: (0, 0)),
            pl.BlockSpec((D, bn), lambda j: (0, j)),
            pl.BlockSpec((1, bn), lambda j: (0, j)),
            pl.BlockSpec((depth, 1, bn), lambda j: (0, 0, j)),
        ],
        out_specs=pl.BlockSpec((depth, B, bn), lambda j: (0, 0, j)),
        out_shape=jax.ShapeDtypeStruct((depth, B, N), F32),
        compiler_params=_params("parallel"),
        name="ada_mod",
    )(c, w_ada, b_ada.reshape(1, N), ada_table.reshape(depth, 1, N))


def _norm_mod_kernel(x_ref, g_ref, sc_ref, sh_ref, o_ref):
    x = x_ref[0]
    y = x * lax.rsqrt(jnp.mean(x * x, axis=-1, keepdims=True) + EPS)
    o_ref[0] = ((y * g_ref[...]) * (1.0 + sc_ref[0]) + sh_ref[0]).astype(o_ref.dtype)


def _norm_mod(x, g, scale, shift):
    B, S, D = x.shape
    tm = _tile(S, 256)
    return pl.pallas_call(
        _norm_mod_kernel,
        grid=(B, S // tm),
        in_specs=[
            pl.BlockSpec((1, tm, D), lambda b, i: (b, i, 0)),
            pl.BlockSpec((1, D), lambda b, i: (0, 0)),
            pl.BlockSpec((1, 1, D), lambda b, i: (b, 0, 0)),
            pl.BlockSpec((1, 1, D), lambda b, i: (b, 0, 0)),
        ],
        out_specs=pl.BlockSpec((1, tm, D), lambda b, i: (b, i, 0)),
        out_shape=jax.ShapeDtypeStruct((B, S, D), BF16),
        compiler_params=_params("parallel", "parallel"),
        name="norm_mod",
    )(x, g.reshape(1, D), scale.reshape(B, 1, D), shift.reshape(B, 1, D))


def _mm_glu_kernel(h_ref, wa_ref, wg_ref, ba_ref, bg_ref, o_ref):
    h = h_ref[...]
    a = _dot(h, wa_ref[...]) + ba_ref[...]
    gt = _dot(h, wg_ref[...]) + bg_ref[...]
    o_ref[...] = a * _sigmoid(gt)


def _mm_glu(h, w, b):
    M, K = h.shape
    N = w.shape[1] // 2
    bm, bn = _tile(M, 1024), _tile(N, 512)
    nj = N // bn
    b2 = b.reshape(1, 2 * N)
    return pl.pallas_call(
        _mm_glu_kernel,
        grid=(M // bm, nj),
        in_specs=[
            pl.BlockSpec((bm, K), lambda i, j: (i, 0)),
            pl.BlockSpec((K, bn), lambda i, j: (0, j)),
            pl.BlockSpec((K, bn), lambda i, j: (0, j + nj)),
            pl.BlockSpec((1, bn), lambda i, j: (0, j)),
            pl.BlockSpec((1, bn), lambda i, j: (0, j + nj)),
        ],
        out_specs=pl.BlockSpec((bm, bn), lambda i, j: (i, j)),
        out_shape=jax.ShapeDtypeStruct((M, N), F32),
        compiler_params=_params("parallel", "parallel"),
        name="mm_glu",
    )(h, w, w, b2, b2)


def _mm_resid_kernel(h_ref, w_ref, b_ref, x_ref, g_ref, o_ref):
    y = _dot(h_ref[...], w_ref[...]) + b_ref[...]
    o_ref[...] = x_ref[...] + g_ref[0] * y


def _mm_resid(h, w, bias, xres, gate, rows_per_batch):
    M, K = h.shape
    N = w.shape[1]
    B = gate.shape[0]
    bm, bn = _tile(rows_per_batch, 1024), _tile(N, 1024)
    nb = rows_per_batch // bm
    return pl.pallas_call(
        _mm_resid_kernel,
        grid=(M // bm, N // bn),
        in_specs=[
            pl.BlockSpec((bm, K), lambda i, j: (i, 0)),
            pl.BlockSpec((K, bn), lambda i, j: (0, j)),
            pl.BlockSpec((1, bn), lambda i, j: (0, j)),
            pl.BlockSpec((bm, bn), lambda i, j: (i, j)),
            pl.BlockSpec((1, 1, bn), lambda i, j: (i // nb, 0, j)),
        ],
        out_specs=pl.BlockSpec((bm, bn), lambda i, j: (i, j)),
        out_shape=jax.ShapeDtypeStruct((M, N), F32),
        compiler_params=_params("parallel", "parallel"),
        name="mm_resid",
    )(h, w, bias.reshape(1, N), xres, gate.reshape(B, 1, N))


def _mm_relu2_kernel(h_ref, w_ref, o_ref):
    y = jnp.maximum(_dot(h_ref[...], w_ref[...]), 0.0)
    o_ref[...] = (y * y).astype(o_ref.dtype)


def _mm_relu2(h, w):
    M, K = h.shape
    N = w.shape[1]
    bm, bn = _tile(M, 1024), _tile(N, 1024)
    return pl.pallas_call(
        _mm_relu2_kernel,
        grid=(M // bm, N // bn),
        in_specs=[
            pl.BlockSpec((bm, K), lambda i, j: (i, 0)),
            pl.BlockSpec((K, bn), lambda i, j: (0, j)),
        ],
        out_specs=pl.BlockSpec((bm, bn), lambda i, j: (i, j)),
        out_shape=jax.ShapeDtypeStruct((M, N), BF16),
        compiler_params=_params("parallel", "parallel"),
        name="mm_relu2",
    )(h, w)


def _mm_kacc_resid_kernel(h_ref, w_ref, x_ref, g_ref, o_ref, acc_ref):
    k = pl.program_id(2)

    @pl.when(k == 0)
    def _():
        acc_ref[...] = jnp.zeros_like(acc_ref)

    acc_ref[...] += _dot(h_ref[...], w_ref[...])

    @pl.when(k == pl.num_programs(2) - 1)
    def _():
        o_ref[...] = x_ref[...] + g_ref[0] * acc_ref[...]


def _mm_kacc_resid(h, w, xres, gate, rows_per_batch):
    M, K = h.shape
    N = w.shape[1]
    B = gate.shape[0]
    bm, bn, bk = _tile(rows_per_batch, 1024), _tile(N, 1024), _tile(K, 2048)
    nb = rows_per_batch // bm
    return pl.pallas_call(
        _mm_kacc_resid_kernel,
        grid=(M // bm, N // bn, K // bk),
        in_specs=[
            pl.BlockSpec((bm, bk), lambda i, j, k: (i, k)),
            pl.BlockSpec((bk, bn), lambda i, j, k: (k, j)),
            pl.BlockSpec((bm, bn), lambda i, j, k: (i, j)),
            pl.BlockSpec((1, 1, bn), lambda i, j, k: (i // nb, 0, j)),
        ],
        out_specs=pl.BlockSpec((bm, bn), lambda i, j, k: (i, j)),
        out_shape=jax.ShapeDtypeStruct((M, N), F32),
        scratch_shapes=[pltpu.VMEM((bm, bn), F32)],
        compiler_params=_params("parallel", "parallel", "arbitrary"),
        name="mm_kacc_resid",
    )(h, w, xres, gate.reshape(B, 1, N))


def _mm_q_kernel(h_ref, w_ref, g_ref, o_ref, *, dh, scale):
    acc = _dot(h_ref[...], w_ref[...])
    g = g_ref[...] * scale
    for c in range(acc.shape[1] // dh):
        blk = acc[:, c * dh:(c + 1) * dh]
        r = lax.rsqrt(jnp.mean(blk * blk, axis=-1, keepdims=True) + EPS)
        o_ref[:, c * dh:(c + 1) * dh] = (blk * r * g).astype(o_ref.dtype)


def _mm_q(h, w, q_gain, dh):
    M, K = h.shape
    N = w.shape[1]
    bm, bn = _tile(M, 1024), _tile(N, 1024)
    return pl.pallas_call(
        functools.partial(_mm_q_kernel, dh=dh, scale=float(dh) ** -0.5),
        grid=(M // bm, N // bn),
        in_specs=[
            pl.BlockSpec((bm, K), lambda i, j: (i, 0)),
            pl.BlockSpec((K, bn), lambda i, j: (0, j)),
            pl.BlockSpec((1, dh), lambda i, j: (0, 0)),
        ],
        out_specs=pl.BlockSpec((bm, bn), lambda i, j: (i, j)),
        out_shape=jax.ShapeDtypeStruct((M, N), BF16),
        compiler_params=_params("parallel", "parallel"),
        name="mm_q",
    )(h, w, q_gain.reshape(1, dh))


def _mm_gate_kernel(h_ref, w_ref, o_ref):
    o_ref[...] = _sigmoid(_dot(h_ref[...], w_ref[...]))


def _mm_gate(h, w):
    M, K = h.shape
    N = w.shape[1]
    bm = _tile(M, 1024)
    return pl.pallas_call(
        _mm_gate_kernel,
        grid=(M // bm,),
        in_specs=[
            pl.BlockSpec((bm, K), lambda i: (i, 0)),
            pl.BlockSpec((K, N), lambda i: (0, 0)),
        ],
        out_specs=pl.BlockSpec((bm, N), lambda i: (i, 0)),
        out_shape=jax.ShapeDtypeStruct((M, N), F32),
        compiler_params=_params("parallel"),
        name="mm_gate",
    )(h, w)


def _mm_kv_kernel(h_ref, w_ref, gain_ref, flag_ref, o_ref, *, dh):
    acc = _dot(h_ref[...], w_ref[...])
    gain = gain_ref[0]
    use_norm = flag_ref[0] > 0.5
    for c in range(o_ref.shape[0]):
        blk = acc[:, c * dh:(c + 1) * dh]
        r = lax.rsqrt(jnp.mean(blk * blk, axis=-1, keepdims=True) + EPS)
        o_ref[c] = jnp.where(use_norm, blk * r * gain, blk).astype(o_ref.dtype)


def _mm_kv(h, w, gain, flag, G, dh):
    M, K = h.shape
    nbr = w.shape[1] // (G * dh)
    bm, bn = _tile(M, 1024), G * dh
    return pl.pallas_call(
        functools.partial(_mm_kv_kernel, dh=dh),
        grid=(M // bm, nbr),
        in_specs=[
            pl.BlockSpec((bm, K), lambda i, j: (i, 0)),
            pl.BlockSpec((K, bn), lambda i, j: (0, j)),
            pl.BlockSpec((1, 1, dh), lambda i, j: (j, 0, 0)),
            pl.BlockSpec((1, 1, dh), lambda i, j: (j, 0, 0)),
        ],
        out_specs=pl.BlockSpec((G, bm, dh), lambda i, j: (j, i, 0)),
        out_shape=jax.ShapeDtypeStruct((nbr * G, M, dh), BF16),
        compiler_params=_params("parallel", "parallel"),
        name="mm_kv",
    )(h, w, gain, flag)


def _dwconv_kernel(ucur_ref, uprev_ref, w_ref, bdw_ref, lng_ref, lnb_ref, o_ref, ext_ref, y_ref, *, tt, kc, D):
    i = pl.program_id(1)
    n = tt + CONV_HALO
    ext_ref[0:CONV_HALO, :] = jnp.where(i > 0, uprev_ref[0], 0.0)
    ext_ref[CONV_HALO:n, :] = ucur_ref[0]
    first = CONV_HALO - (kc - 1)
    nchunk = D // LANES

    def conv_chunk(c, carry):
        s1, s2 = carry
        col = pl.ds(pl.multiple_of(c * LANES, LANES), LANES)
        big = ext_ref[:, col]
        shifted = [big] + [pltpu.roll(big, n - s, axis=0) for s in range(1, SUBLANES)]
        acc = jnp.zeros((tt, LANES), F32)
        for k in range(kc):
            a, s = divmod(first + k, SUBLANES)
            acc = acc + w_ref[k:k + 1, col] * shifted[s][a * SUBLANES:a * SUBLANES + tt]
        y = acc + bdw_ref[:, col]
        y_ref[:, col] = y
        return s1 + y, s2 + y * y

    zero = jnp.zeros((tt, LANES), F32)
    s1, s2 = lax.fori_loop(0, nchunk, conv_chunk, (zero, zero))
    mean = jnp.sum(s1, axis=-1, keepdims=True) * (1.0 / D)
    var = jnp.maximum(jnp.sum(s2, axis=-1, keepdims=True) * (1.0 / D) - mean * mean, 0.0)
    rstd = lax.rsqrt(var + EPS)

    def norm_chunk(c, carry):
        col = pl.ds(pl.multiple_of(c * LANES, LANES), LANES)
        yn = (y_ref[:, col] - mean) * rstd * lng_ref[:, col] + lnb_ref[:, col]
        o_ref[0, :, col] = (yn * _sigmoid(yn)).astype(o_ref.dtype)
        return carry

    lax.fori_loop(0, nchunk, norm_chunk, 0)


def _dwconv_ln_silu(u, w_dw, b_dw, ln_g, ln_b):
    B, S, D = u.shape
    kc = w_dw.shape[0]
    assert kc - 1 <= CONV_HALO and D % LANES == 0
    tt = _tile(S, 128)
    assert tt % CONV_HALO == 0
    hb = tt // CONV_HALO
    return pl.pallas_call(
        functools.partial(_dwconv_kernel, tt=tt, kc=kc, D=D),
        grid=(B, S // tt),
        in_specs=[
            pl.BlockSpec((1, tt, D), lambda b, i: (b, i, 0)),
            pl.BlockSpec((1, CONV_HALO, D), lambda b, i: (b, jnp.maximum(i * hb - 1, 0), 0)),
            pl.BlockSpec((kc, D), lambda b, i: (0, 0)),
            pl.BlockSpec((1, D), lambda b, i: (0, 0)),
            pl.BlockSpec((1, D), lambda b, i: (0, 0)),
            pl.BlockSpec((1, D), lambda b, i: (0, 0)),
        ],
        out_specs=pl.BlockSpec((1, tt, D), lambda b, i: (b, i, 0)),
        out_shape=jax.ShapeDtypeStruct((B, S, D), BF16),
        scratch_shapes=[pltpu.VMEM((tt + CONV_HALO, D), F32), pltpu.VMEM((tt, D), F32)],
        compiler_params=_params("parallel", "parallel"),
        name="dwconv_ln_silu",
    )(u, u, w_dw, b_dw.reshape(1, D), ln_g.reshape(1, D), ln_b.reshape(1, D))


def _compress_kernel(t_ref, pos_ref, w1a_ref, w1b_ref, b1_ref, w2_ref, b2_ref, g_ref, o_ref):
    br = pl.program_id(0)
    t = t_ref[0].astype(F32)
    n = t.shape[0]
    pos = pos_ref[0]
    a = _dot((t + pos[0:1]).astype(BF16), w1a_ref[0])
    b = _dot((t + pos[1:2]).astype(BF16), w1b_ref[0])
    hid = a + pltpu.roll(b, n - 1, axis=0) + b1_ref[0]
    hid = hid * _sigmoid(hid)
    out = _dot(hid.astype(BF16), w2_ref[0]) + b2_ref[0]
    normed = out * lax.rsqrt(jnp.mean(out * out, axis=-1, keepdims=True) + EPS) * g_ref[...]
    out = jnp.where(br == 0, normed, out)
    row = lax.broadcasted_iota(jnp.int32, out.shape, 0)
    o_ref[0, 0, 0] = jnp.where(row < n - 1, out, 0.0).astype(o_ref.dtype)


def _compress(kvh, cmp_pos, cmp_w1, cmp_b1, cmp_w2, cmp_b2, k_gain0, B, G, S, dh):
    blk = cmp_pos.shape[1]
    assert blk == 2 * CMP_STRIDE
    half = CMP_STRIDE * dh
    n = S // CMP_STRIDE
    hid = cmp_w1.shape[2]
    t = kvh[:2 * G].reshape(2 * G, B * n, half)
    pos = cmp_pos.reshape(2, 2, half)
    w1 = cmp_w1.astype(BF16)
    return pl.pallas_call(
        _compress_kernel,
        grid=(2, B, G),
        in_specs=[
            pl.BlockSpec((1, n, half), lambda i, b, g: (i * G + g, b, 0)),
            pl.BlockSpec((1, 2, half), lambda i, b, g: (i, 0, 0)),
            pl.BlockSpec((1, half, hid), lambda i, b, g: (i, 0, 0)),
            pl.BlockSpec((1, half, hid), lambda i, b, g: (i, 1, 0)),
            pl.BlockSpec((1, 1, hid), lambda i, b, g: (i, 0, 0)),
            pl.BlockSpec((1, hid, dh), lambda i, b, g: (i, 0, 0)),
            pl.BlockSpec((1, 1, dh), lambda i, b, g: (i, 0, 0)),
            pl.BlockSpec((1, dh), lambda i, b, g: (0, 0)),
        ],
        out_specs=pl.BlockSpec((1, 1, 1, n, dh), lambda i, b, g: (i, b, g, 0, 0)),
        out_shape=jax.ShapeDtypeStruct((2, B, G, n, dh), BF16),
        compiler_params=_params("parallel", "parallel", "parallel"),
        name="compress_kv",
    )(t, pos, w1, w1, cmp_b1.reshape(2, 1, hid), cmp_w2.astype(BF16), cmp_b2.reshape(2, 1, dh),
      k_gain0.reshape(1, dh))


def _softmax_rows(s, mask):
    s = jnp.where(mask, s, NEG)
    m = jnp.max(s, axis=-1, keepdims=True)
    p = jnp.exp(s - m) * mask.astype(F32)
    return p / jnp.maximum(jnp.sum(p, axis=-1, keepdims=True), TINY)


def _nsa_kernel(q_ref, gate_ref, kc_ref, vc_ref, ks_ref, vs_ref, kw_ref, vw_ref, wsel_ref, eexp_ref, o_ref,
                st_ref, *, T, R, dh, S, cmp_blk):
    q0 = pl.program_id(2) * T
    q = q_ref[0]
    qs = jnp.concatenate([q[:, r * dh:(r + 1) * dh] for r in range(R)], axis=0)
    t_pos = q0 + lax.broadcasted_iota(jnp.int32, (T, 1), 0)

    kc = kc_ref[0, 0, 0]
    ncp = kc.shape[0]
    s_c = _dot_nt(qs, kc)
    cmp_end = lax.broadcasted_iota(jnp.int32, (1, ncp), 1) * CMP_STRIDE + (cmp_blk - 1)
    mask_c = cmp_end <= t_pos
    p_c = [_softmax_rows(s_c[r * T:(r + 1) * T], mask_c) for r in range(R)]
    o_c = _dot(jnp.concatenate(p_c, axis=0).astype(BF16), vc_ref[0, 0, 0])

    p_sum = p_c[0]
    for r in range(1, R):
        p_sum = p_sum + p_c[r]
    imp = _dot(p_sum.astype(BF16), wsel_ref[...])
    ns = imp.shape[1]
    jblk = lax.broadcasted_iota(jnp.int32, (1, ns), 1)
    cur = t_pos // SEL_BLK
    forced = (jblk == 0) | (jblk == cur) | (jblk == cur - 1)
    score = jnp.where(forced, BIG, imp)
    score = jnp.where(jblk <= cur, score, -BIG)
    st = score.T
    st_ref[...] = st
    jrow = lax.broadcasted_iota(jnp.int32, (ns, T), 0)

    def rank_step(i, cnt):
        row = st_ref[pl.ds(i, 1), :]
        ahead = (row > st) | ((row == st) & (i < jrow))
        return cnt + ahead.astype(F32)

    rank = lax.fori_loop(0, ns, rank_step, jnp.zeros((ns, T), F32))
    sel = (rank < float(min(N_SEL, ns))).astype(F32).T.astype(BF16)

    tk = min(ATT_KV_TILE, S)

    def kv_step(kt, carry):
        m, l, acc = carry
        k0 = pl.multiple_of(kt * tk, tk)
        s = _dot_nt(qs, ks_ref[0, pl.ds(k0, tk), :])
        chosen = _dot(sel, eexp_ref[:, pl.ds(k0, tk)])
        kpos = k0 + lax.broadcasted_iota(jnp.int32, (1, tk), 1)
        bias = jnp.where((chosen > 0.5) & (kpos <= t_pos), 0.0, NEG)
        m_new, l_new, p = [], [], []
        for r in range(R):
            rows = slice(r * T, (r + 1) * T)
            s_r = s[rows] + bias
            m_r = jnp.maximum(m[rows], jnp.max(s_r, axis=-1, keepdims=True))
            p_r = jnp.exp(s_r - m_r)
            l_new.append(jnp.exp(m[rows] - m_r) * l[rows] + jnp.sum(p_r, axis=-1, keepdims=True))
            m_new.append(m_r)
            p.append(p_r.astype(BF16))
        m_new = jnp.concatenate(m_new, axis=0)
        pv = _dot(jnp.concatenate(p, axis=0), vs_ref[0, pl.ds(k0, tk), :])
        return m_new, jnp.concatenate(l_new, axis=0), jnp.exp(m - m_new) * acc + pv

    n_tiles = (q0 + T + tk - 1) // tk
    init = (jnp.full((R * T, 1), NEG, F32), jnp.zeros((R * T, 1), F32), jnp.zeros((R * T, dh), F32))
    _, l_s, acc_s = lax.fori_loop(0, n_tiles, kv_step, init)
    o_s = acc_s / jnp.maximum(l_s, TINY)

    nw = min(WINDOW + T, S)
    w0 = pl.multiple_of(jnp.clip(q0 - WINDOW, 0, S - nw), T)
    s_w = _dot_nt(qs, kw_ref[0, pl.ds(w0, nw), :])
    diff = t_pos - (w0 + lax.broadcasted_iota(jnp.int32, (1, nw), 1))
    mask_w = (diff >= 0) & (diff < WINDOW)
    p_w = [_softmax_rows(s_w[r * T:(r + 1) * T], mask_w) for r in range(R)]
    o_w = _dot(jnp.concatenate(p_w, axis=0).astype(BF16), vw_ref[0, pl.ds(w0, nw), :])

    gates = gate_ref[0]
    out = []
    for r in range(R):
        rows = slice(r * T, (r + 1) * T)
        c0 = N_BRANCH * r
        out.append(gates[:, c0:c0 + 1] * o_c[rows] + gates[:, c0 + 1:c0 + 2] * o_s[rows]
                   + gates[:, c0 + 2:c0 + 3] * o_w[rows])
    o_ref[0] = jnp.concatenate(out, axis=1).astype(o_ref.dtype)


def _nsa_attention(q, gates, cmp_kv, kvh, wsel, eexp, B, G, R, S, dh, cmp_blk):
    T = _tile(S, ATT_Q_TILE)
    ncp, ns = wsel.shape
    def kv_spec(br):
        return pl.BlockSpec((1, S, dh), lambda b, g, i: (br * G + g, b, 0))

    def cmp_spec(br):
        return pl.BlockSpec((1, 1, 1, ncp, dh), lambda b, g, i: (br, b, g, 0, 0))

    return pl.pallas_call(
        functools.partial(_nsa_kernel, T=T, R=R, dh=dh, S=S, cmp_blk=cmp_blk),
        grid=(B, G, S // T),
        in_specs=[
            pl.BlockSpec((1, T, R * dh), lambda b, g, i: (b, i, g)),
            pl.BlockSpec((1, T, LANES), lambda b, g, i: (b, i, g)),
            cmp_spec(0), cmp_spec(1),
            kv_spec(2), kv_spec(3), kv_spec(4), kv_spec(5),
            pl.BlockSpec((ncp, ns), lambda b, g, i: (0, 0)),
            pl.BlockSpec((ns, S), lambda b, g, i: (0, 0)),
        ],
        out_specs=pl.BlockSpec((1, T, R * dh), lambda b, g, i: (b, i, g)),
        out_shape=jax.ShapeDtypeStruct((B, S, G * R * dh), BF16),
        scratch_shapes=[pltpu.VMEM((ns, T), F32)],
        compiler_params=_params("parallel", "parallel", "arbitrary"),
        name="nsa_attention",
    )(q, gates, cmp_kv, cmp_kv, kvh, kvh, kvh, kvh, wsel, eexp)


def _selection_constants(S, ncp, cmp_blk):
    ns = S // SEL_BLK
    nc = (S - cmp_blk) // CMP_STRIDE + 1
    cs = jnp.arange(ncp) * CMP_STRIDE
    ss = jnp.arange(ns) * SEL_BLK
    ov = jnp.clip(jnp.minimum(cs[:, None] + cmp_blk, ss[None, :] + SEL_BLK) - jnp.maximum(cs[:, None], ss[None, :]),
                  0, None)
    wsel = jnp.where(jnp.arange(ncp)[:, None] < nc, ov / CMP_STRIDE, 0.0).astype(BF16)
    eexp = (jnp.arange(S)[None, :] // SEL_BLK == jnp.arange(ns)[:, None]).astype(BF16)
    return wsel, eexp


def kernel(x, c, w_ada, b_ada, ada_table, norm_mix_g, norm_ffn_g, w_ffn1, w_ffn2,
           conv_w_pw1, conv_b_pw1, conv_w_dw, conv_b_dw, conv_ln_g, conv_ln_b, conv_w_pw2, conv_b_pw2,
           kv_norm_g, w_kv, k_norm_g, cmp_pos, cmp_w1, cmp_b1, cmp_w2, cmp_b2,
           attn_w_qg, attn_q_norm_g, attn_w_o):
    B, S, D = x.shape
    M = B * S
    depth = ada_table.shape[0]
    n_a = conv_w_pw1.shape[0]
    dh = k_norm_g.shape[1]
    G = w_kv.shape[1] // (2 * N_BRANCH * dh)
    HD = attn_w_o.shape[1]
    R = HD // (G * dh)
    cmp_blk = cmp_pos.shape[1]
    assert R * N_BRANCH <= LANES and S % SEL_BLK == 0 and S % CMP_STRIDE == 0

    mod = _ada_mod(c, w_ada, b_ada, ada_table).reshape(depth, B, N_MOD, D)
    x2 = x.reshape(M, D)
    zeros_bd = jnp.zeros((B, D), F32)
    kvh = cmp_kv = None
    wsel, eexp = _selection_constants(S, S // CMP_STRIDE, cmp_blk)

    for l in range(depth):
        sh_m, sc_m, g_m, sh_f, sc_f, g_f = (mod[l, :, i] for i in range(N_MOD))
        h = _norm_mod(x2.reshape(B, S, D), norm_mix_g[l], sc_m, sh_m)
        if l < n_a:
            u = _mm_glu(h.reshape(M, D), conv_w_pw1[l].astype(BF16), conv_b_pw1[l])
            v = _dwconv_ln_silu(u.reshape(B, S, D), conv_w_dw[l], conv_b_dw[l], conv_ln_g[l], conv_ln_b[l])
            x2 = _mm_resid(v.reshape(M, D), conv_w_pw2[l].astype(BF16), conv_b_pw2[l], x2, g_m, S)
        else:
            i = l - n_a
            hm = h.reshape(M, D)
            q = _mm_q(hm, attn_w_qg[i][:, :HD].astype(BF16), attn_q_norm_g[i], dh)
            wg = attn_w_qg[i][:, HD:].reshape(D, G, R * N_BRANCH)
            wg = jnp.pad(wg, ((0, 0), (0, 0), (0, LANES - R * N_BRANCH))).reshape(D, G * LANES)
            gates = _mm_gate(hm, wg.astype(BF16))
            o = _nsa_attention(q.reshape(B, S, HD), gates.reshape(B, S, G * LANES), cmp_kv, kvh, wsel, eexp,
                               B, G, R, S, dh, cmp_blk)
            x2 = _mm_resid(o.reshape(M, HD), attn_w_o[i].astype(BF16), jnp.zeros((D,), F32), x2, g_m, S)
        h = _norm_mod(x2.reshape(B, S, D), norm_ffn_g[l], sc_f, sh_f)
        hid = _mm_relu2(h.reshape(M, D), w_ffn1[l].astype(BF16))
        x2 = _mm_kacc_resid(hid, w_ffn2[l].astype(BF16), x2, g_f, S)
        if l == n_a - 1:
            hk = _norm_mod(x2.reshape(B, S, D), kv_norm_g, zeros_bd, zeros_bd)
            ones = jnp.ones((dh,), F32)
            gain = jnp.stack([ones, ones, k_norm_g[1], ones, k_norm_g[2], ones]).reshape(2 * N_BRANCH, 1, dh)
            flag = jnp.array([0.0, 0.0, 1.0, 0.0, 1.0, 0.0], F32)[:, None, None] * jnp.ones((1, 1, dh), F32)
            kvh = _mm_kv(hk.reshape(M, D), w_kv.astype(BF16), gain, flag, G, dh)
            cmp_kv = _compress(kvh, cmp_pos, cmp_w1, cmp_b1, cmp_w2, cmp_b2, k_norm_g[0], B, G, S, dh)
    return x2.reshape(B, S, D)
```

```python
import functools

import jax
import jax.numpy as jnp
from jax import lax
from jax.experimental import pallas as pl
from jax.experimental.pallas import tpu as pltpu

F32 = jnp.float32
BF16 = jnp.bfloat16

CMP_STRIDE = 16
SEL_BLK = 64
N_SEL = 16
WINDOW = 512
N_BRANCH = 3
N_MOD = 6
EPS = 1e-6
NEG = -1e30
BIG = 1e9
TINY = 1e-20
LOG2E = 1.4426950408889634

LANES = 128
SUBLANES = 8
VMEM_LIMIT_BYTES = 56 * 1024 * 1024

ATT_Q_TILE = 128
ATT_KV_TILE = 1024
ATT_HEAD_GROUP = 8
CONV_HALO = 32


def _tile(dim, pref):
    t = min(dim, pref)
    assert dim % t == 0, (dim, pref)
    return t


def _params(*sem):
    return pltpu.CompilerParams(dimension_semantics=sem, vmem_limit_bytes=VMEM_LIMIT_BYTES)


def _dot(a, b):
    return jnp.dot(a, b, preferred_element_type=F32)


def _sigmoid(x):
    return 1.0 / (1.0 + jnp.exp(-x))


def _ada_kernel(c_ref, w_ref, b_ref, tab_ref, o_ref):
    c = c_ref[...]
    a = (c * _sigmoid(c)).astype(BF16)
    base = _dot(a, w_ref[...].astype(BF16)) + b_ref[...]
    for l in range(o_ref.shape[0]):
        o_ref[l] = base + tab_ref[l]


def _ada_mod(c, w_ada, b_ada, ada_table):
    B, D = c.shape
    depth = ada_table.shape[0]
    N = w_ada.shape[1]
    bn = _tile(N, 1024)
    return pl.pallas_call(
        _ada_kernel,
        grid=(N // bn,),
        in_specs=[
            pl.BlockSpec((B, D), lambda j: (0, 0)),
            pl.BlockSpec((D, bn), lambda j: (0, j)),
            pl.BlockSpec((1, bn), lambda j: (0, j)),
            pl.BlockSpec((depth, 1, bn), lambda j: (0, 0, j)),
        ],
        out_specs=pl.BlockSpec((depth, B, bn), lambda j: (0, 0, j)),
        out_shape=jax.ShapeDtypeStruct((depth, B, N), F32),
        compiler_params=_params("parallel"),
        name="ada_mod",
    )(c, w_ada, b_ada.reshape(1, N), ada_table.reshape(depth, 1, N))


def _norm_mod_kernel(x_ref, g_ref, sc_ref, sh_ref, o_ref):
    x = x_ref[0]
    y = x * lax.rsqrt(jnp.mean(x * x, axis=-1, keepdims=True) + EPS)
    o_ref[0] = ((y * g_ref[...]) * (1.0 + sc_ref[0]) + sh_ref[0]).astype(o_ref.dtype)


def _norm_mod(x, g, scale, shift):
    B, S, D = x.shape
    tm = _tile(S, 256)
    return pl.pallas_call(
        _norm_mod_kernel,
        grid=(B, S // tm),
        in_specs=[
            pl.BlockSpec((1, tm, D), lambda b, i: (b, i, 0)),
            pl.BlockSpec((1, D), lambda b, i: (0, 0)),
            pl.BlockSpec((1, 1, D), lambda b, i: (b, 0, 0)),
            pl.BlockSpec((1, 1, D), lambda b, i: (b, 0, 0)),
        ],
        out_specs=pl.BlockSpec((1, tm, D), lambda b, i: (b, i, 0)),
        out_shape=jax.ShapeDtypeStruct((B, S, D), BF16),
        compiler_params=_params("parallel", "parallel"),
        name="norm_mod",
    )(x, g.reshape(1, D), scale.reshape(B, 1, D), shift.reshape(B, 1, D))


def _mm_glu_kernel(h_ref, wa_ref, wg_ref, ba_ref, bg_ref, o_ref):
    h = h_ref[...]
    a = _dot(h, wa_ref[...]) + ba_ref[...]
    gt = _dot(h, wg_ref[...]) + bg_ref[...]
    o_ref[...] = a * _sigmoid(gt)


def _mm_glu(h, w, b):
    M, K = h.shape
    N = w.shape[1] // 2
    bm, bn = _tile(M, 1024), _tile(N, 512)
    nj = N // bn
    b2 = b.reshape(1, 2 * N)
    return pl.pallas_call(
        _mm_glu_kernel,
        grid=(M // bm, nj),
        in_specs=[
            pl.BlockSpec((bm, K), lambda i, j: (i, 0)),
            pl.BlockSpec((K, bn), lambda i, j: (0, j)),
            pl.BlockSpec((K, bn), lambda i, j: (0, j + nj)),
            pl.BlockSpec((1, bn), lambda i, j: (0, j)),
            pl.BlockSpec((1, bn), lambda i, j: (0, j + nj)),
        ],
        out_specs=pl.BlockSpec((bm, bn), lambda i, j: (i, j)),
        out_shape=jax.ShapeDtypeStruct((M, N), F32),
        compiler_params=_params("parallel", "parallel"),
        name="mm_glu",
    )(h, w, w, b2, b2)


def _mm_resid_kernel(h_ref, w_ref, b_ref, x_ref, g_ref, o_ref):
    y = _dot(h_ref[...], w_ref[...]) + b_ref[...]
    o_ref[...] = x_ref[...] + g_ref[0] * y


def _mm_resid(h, w, bias, xres, gate, rows_per_batch):
    M, K = h.shape
    N = w.shape[1]
    B = gate.shape[0]
    bm, bn = _tile(rows_per_batch, 1024), _tile(N, 1024)
    nb = rows_per_batch // bm
    return pl.pallas_call(
        _mm_resid_kernel,
        grid=(M // bm, N // bn),
        in_specs=[
            pl.BlockSpec((bm, K), lambda i, j: (i, 0)),
            pl.BlockSpec((K, bn), lambda i, j: (0, j)),
            pl.BlockSpec((1, bn), lambda i, j: (0, j)),
            pl.BlockSpec((bm, bn), lambda i, j: (i, j)),
            pl.BlockSpec((1, 1, bn), lambda i, j: (i // nb, 0, j)),
        ],
        out_specs=pl.BlockSpec((bm, bn), lambda i, j: (i, j)),
        out_shape=jax.ShapeDtypeStruct((M, N), F32),
        compiler_params=_params("parallel", "parallel"),
        name="mm_resid",
    )(h, w, bias.reshape(1, N), xres, gate.reshape(B, 1, N))


def _mm_relu2_kernel(h_ref, w_ref, o_ref):
    y = jnp.maximum(_dot(h_ref[...], w_ref[...]), 0.0)
    o_ref[...] = (y * y).astype(o_ref.dtype)


def _mm_relu2(h, w):
    M, K = h.shape
    N = w.shape[1]
    bm, bn = _tile(M, 1024), _tile(N, 1024)
    return pl.pallas_call(
        _mm_relu2_kernel,
        grid=(M // bm, N // bn),
        in_specs=[
            pl.BlockSpec((bm, K), lambda i, j: (i, 0)),
            pl.BlockSpec((K, bn), lambda i, j: (0, j)),
        ],
        out_specs=pl.BlockSpec((bm, bn), lambda i, j: (i, j)),
        out_shape=jax.ShapeDtypeStruct((M, N), BF16),
        compiler_params=_params("parallel", "parallel"),
        name="mm_relu2",
    )(h, w)


def _mm_kacc_resid_kernel(h_ref, w_ref, x_ref, g_ref, o_ref, acc_ref):
    k = pl.program_id(2)

    @pl.when(k == 0)
    def _():
        acc_ref[...] = jnp.zeros_like(acc_ref)

    acc_ref[...] += _dot(h_ref[...], w_ref[...])

    @pl.when(k == pl.num_programs(2) - 1)
    def _():
        o_ref[...] = x_ref[...] + g_ref[0] * acc_ref[...]


def _mm_kacc_resid(h, w, xres, gate, rows_per_batch):
    M, K = h.shape
    N = w.shape[1]
    B = gate.shape[0]
    bm, bn, bk = _tile(rows_per_batch, 1024), _tile(N, 1024), _tile(K, 2048)
    nb = rows_per_batch // bm
    return pl.pallas_call(
        _mm_kacc_resid_kernel,
        grid=(M // bm, N // bn, K // bk),
        in_specs=[
            pl.BlockSpec((bm, bk), lambda i, j, k: (i, k)),
            pl.BlockSpec((bk, bn), lambda i, j, k: (k, j)),
            pl.BlockSpec((bm, bn), lambda i, j, k: (i, j)),
            pl.BlockSpec((1, 1, bn), lambda i, j, k: (i // nb, 0, j)),
        ],
        out_specs=pl.BlockSpec((bm, bn), lambda i, j, k: (i, j)),
        out_shape=jax.ShapeDtypeStruct((M, N), F32),
        scratch_shapes=[pltpu.VMEM((bm, bn), F32)],
        compiler_params=_params("parallel", "parallel", "arbitrary"),
        name="mm_kacc_resid",
    )(h, w, xres, gate.reshape(B, 1, N))


def _mm_q_kernel(h_ref, w_ref, g_ref, o_ref, *, dh, scale):
    acc = _dot(h_ref[...], w_ref[...])
    g = g_ref[...] * scale
    for c in range(acc.shape[1] // dh):
        blk = acc[:, c * dh:(c + 1) * dh]
        r = lax.rsqrt(jnp.mean(blk * blk, axis=-1, keepdims=True) + EPS)
        o_ref[:, c * dh:(c + 1) * dh] = (blk * r * g).astype(o_ref.dtype)


def _mm_q(h, w, q_gain, dh):
    M, K = h.shape
    N = w.shape[1]
    bm, bn = _tile(M, 1024), _tile(N, 1024)
    return pl.pallas_call(
        functools.partial(_mm_q_kernel, dh=dh, scale=float(dh) ** -0.5 * LOG2E),
        grid=(M // bm, N // bn),
        in_specs=[
            pl.BlockSpec((bm, K), lambda i, j: (i, 0)),
            pl.BlockSpec((K, bn), lambda i, j: (0, j)),
            pl.BlockSpec((1, dh), lambda i, j: (0, 0)),
        ],
        out_specs=pl.BlockSpec((bm, bn), lambda i, j: (i, j)),
        out_shape=jax.ShapeDtypeStruct((M, N), BF16),
        compiler_params=_params("parallel", "parallel"),
        name="mm_q",
    )(h, w, q_gain.reshape(1, dh))


def _mm_gate_kernel(h_ref, w_ref, o_ref):
    o_ref[...] = _sigmoid(_dot(h_ref[...], w_ref[...]))


def _mm_gate(h, w):
    M, K = h.shape
    N = w.shape[1]
    bm = _tile(M, 1024)
    return pl.pallas_call(
        _mm_gate_kernel,
        grid=(M // bm,),
        in_specs=[
            pl.BlockSpec((bm, K), lambda i: (i, 0)),
            pl.BlockSpec((K, N), lambda i: (0, 0)),
        ],
        out_specs=pl.BlockSpec((bm, N), lambda i: (i, 0)),
        out_shape=jax.ShapeDtypeStruct((M, N), F32),
        compiler_params=_params("parallel"),
        name="mm_gate",
    )(h, w)


def _mm_kv_kernel(h_ref, w_ref, gain_ref, flag_ref, o_ref, *, dh):
    acc = _dot(h_ref[...], w_ref[...])
    gain = gain_ref[0]
    use_norm = flag_ref[0] > 0.5
    for c in range(o_ref.shape[0]):
        blk = acc[:, c * dh:(c + 1) * dh]
        r = lax.rsqrt(jnp.mean(blk * blk, axis=-1, keepdims=True) + EPS)
        o_ref[c] = jnp.where(use_norm, blk * r * gain, blk).astype(o_ref.dtype)


def _mm_kv(h, w, gain, flag, G, dh):
    M, K = h.shape
    nbr = w.shape[1] // (G * dh)
    bm, bn = _tile(M, 1024), G * dh
    return pl.pallas_call(
        functools.partial(_mm_kv_kernel, dh=dh),
        grid=(M // bm, nbr),
        in_specs=[
            pl.BlockSpec((bm, K), lambda i, j: (i, 0)),
            pl.BlockSpec((K, bn), lambda i, j: (0, j)),
            pl.BlockSpec((1, 1, dh), lambda i, j: (j, 0, 0)),
            pl.BlockSpec((1, 1, dh), lambda i, j: (j, 0, 0)),
        ],
        out_specs=pl.BlockSpec((G, bm, dh), lambda i, j: (j, i, 0)),
        out_shape=jax.ShapeDtypeStruct((nbr * G, M, dh), BF16),
        compiler_params=_params("parallel", "parallel"),
        name="mm_kv",
    )(h, w, gain, flag)


def _dwconv_kernel(ucur_ref, uprev_ref, w_ref, bdw_ref, lng_ref, lnb_ref, o_ref, ext_ref, y_ref, *, tt, kc, D):
    i = pl.program_id(1)
    n = tt + CONV_HALO
    ext_ref[0:CONV_HALO, :] = jnp.where(i > 0, uprev_ref[0], 0.0)
    ext_ref[CONV_HALO:n, :] = ucur_ref[0]
    first = CONV_HALO - (kc - 1)
    nchunk = D // LANES

    def conv_chunk(c, carry):
        s1, s2 = carry
        col = pl.ds(pl.multiple_of(c * LANES, LANES), LANES)
        big = ext_ref[:, col]
        shifted = [big] + [pltpu.roll(big, n - s, axis=0) for s in range(1, SUBLANES)]
        acc = jnp.zeros((tt, LANES), F32)
        for k in range(kc):
            a, s = divmod(first + k, SUBLANES)
            acc = acc + w_ref[k:k + 1, col] * shifted[s][a * SUBLANES:a * SUBLANES + tt]
        y = acc + bdw_ref[:, col]
        y_ref[:, col] = y
        return s1 + y, s2 + y * y

    zero = jnp.zeros((tt, LANES), F32)
    s1, s2 = lax.fori_loop(0, nchunk, conv_chunk, (zero, zero))
    mean = jnp.sum(s1, axis=-1, keepdims=True) * (1.0 / D)
    var = jnp.maximum(jnp.sum(s2, axis=-1, keepdims=True) * (1.0 / D) - mean * mean, 0.0)
    rstd = lax.rsqrt(var + EPS)

    def norm_chunk(c, carry):
        col = pl.ds(pl.multiple_of(c * LANES, LANES), LANES)
        yn = (y_ref[:, col] - mean) * rstd * lng_ref[:, col] + lnb_ref[:, col]
        o_ref[0, :, col] = (yn * _sigmoid(yn)).astype(o_ref.dtype)
        return carry

    lax.fori_loop(0, nchunk, norm_chunk, 0)


def _dwconv_ln_silu(u, w_dw, b_dw, ln_g, ln_b):
    B, S, D = u.shape
    kc = w_dw.shape[0]
    assert kc - 1 <= CONV_HALO and D % LANES == 0
    tt = _tile(S, 128)
    assert tt % CONV_HALO == 0
    hb = tt // CONV_HALO
    return pl.pallas_call(
        functools.partial(_dwconv_kernel, tt=tt, kc=kc, D=D),
        grid=(B, S // tt),
        in_specs=[
            pl.BlockSpec((1, tt, D), lambda b, i: (b, i, 0)),
            pl.BlockSpec((1, CONV_HALO, D), lambda b, i: (b, jnp.maximum(i * hb - 1, 0), 0)),
            pl.BlockSpec((kc, D), lambda b, i: (0, 0)),
            pl.BlockSpec((1, D), lambda b, i: (0, 0)),
            pl.BlockSpec((1, D), lambda b, i: (0, 0)),
            pl.BlockSpec((1, D), lambda b, i: (0, 0)),
        ],
        out_specs=pl.BlockSpec((1, tt, D), lambda b, i: (b, i, 0)),
        out_shape=jax.ShapeDtypeStruct((B, S, D), BF16),
        scratch_shapes=[pltpu.VMEM((tt + CONV_HALO, D), F32), pltpu.VMEM((tt, D), F32)],
        compiler_params=_params("parallel", "parallel"),
        name="dwconv_ln_silu",
    )(u, u, w_dw, b_dw.reshape(1, D), ln_g.reshape(1, D), ln_b.reshape(1, D))


def _compress_kernel(t_ref, pos_ref, w1a_ref, w1b_ref, b1_ref, w2_ref, b2_ref, g_ref, o_ref):
    br = pl.program_id(0)
    t = t_ref[0].astype(F32)
    n = t.shape[0]
    pos = pos_ref[0]
    a = _dot((t + pos[0:1]).astype(BF16), w1a_ref[0])
    b = _dot((t + pos[1:2]).astype(BF16), w1b_ref[0])
    hid = a + pltpu.roll(b, n - 1, axis=0) + b1_ref[0]
    hid = hid * _sigmoid(hid)
    out = _dot(hid.astype(BF16), w2_ref[0]) + b2_ref[0]
    normed = out * lax.rsqrt(jnp.mean(out * out, axis=-1, keepdims=True) + EPS) * g_ref[...]
    out = jnp.where(br == 0, normed, out)
    row = lax.broadcasted_iota(jnp.int32, out.shape, 0)
    o_ref[0, 0, 0] = jnp.where(row < n - 1, out, 0.0).astype(o_ref.dtype)


def _compress(kvh, cmp_pos, cmp_w1, cmp_b1, cmp_w2, cmp_b2, k_gain0, B, G, S, dh):
    blk = cmp_pos.shape[1]
    assert blk == 2 * CMP_STRIDE
    half = CMP_STRIDE * dh
    n = S // CMP_STRIDE
    hid = cmp_w1.shape[2]
    t = kvh[:2 * G].reshape(2 * G, B * n, half)
    pos = cmp_pos.reshape(2, 2, half)
    w1 = cmp_w1.astype(BF16)
    return pl.pallas_call(
        _compress_kernel,
        grid=(2, B, G),
        in_specs=[
            pl.BlockSpec((1, n, half), lambda i, b, g: (i * G + g, b, 0)),
            pl.BlockSpec((1, 2, half), lambda i, b, g: (i, 0, 0)),
            pl.BlockSpec((1, half, hid), lambda i, b, g: (i, 0, 0)),
            pl.BlockSpec((1, half, hid), lambda i, b, g: (i, 1, 0)),
            pl.BlockSpec((1, 1, hid), lambda i, b, g: (i, 0, 0)),
            pl.BlockSpec((1, hid, dh), lambda i, b, g: (i, 0, 0)),
            pl.BlockSpec((1, 1, dh), lambda i, b, g: (i, 0, 0)),
            pl.BlockSpec((1, dh), lambda i, b, g: (0, 0)),
        ],
        out_specs=pl.BlockSpec((1, 1, 1, n, dh), lambda i, b, g: (i, b, g, 0, 0)),
        out_shape=jax.ShapeDtypeStruct((2, B, G, n, dh), BF16),
        compiler_params=_params("parallel", "parallel", "parallel"),
        name="compress_kv",
    )(t, pos, w1, w1, cmp_b1.reshape(2, 1, hid), cmp_w2.astype(BF16), cmp_b2.reshape(2, 1, dh),
      k_gain0.reshape(1, dh))


def _masked_softmax_t(s, mask, maskf):
    s = jnp.where(mask, s, NEG)
    m = jnp.max(s, axis=0, keepdims=True)
    p = jnp.exp2(s - m) * maskf
    return p * (1.0 / jnp.maximum(jnp.sum(p, axis=0, keepdims=True), TINY))


def _nsa_kernel(q_ref, gate_ref, kc_ref, vct_ref, ks_ref, vst_ref, kw_ref, vwt_ref, wselt_ref, et_ref, o_ref,
                *, T, R, dh, S, cmp_blk):
    q0 = pl.program_id(2) * T
    q = q_ref[0].astype(F32)
    qt = jnp.concatenate([q[:, r * dh:(r + 1) * dh].T for r in range(R)], axis=1).astype(BF16)
    t_row = q0 + lax.broadcasted_iota(jnp.int32, (1, T), 1)
    heads = [slice(r * T, (r + 1) * T) for r in range(R)]

    kc = kc_ref[0, 0, 0]
    ncp = kc.shape[0]
    s_c = _dot(kc, qt)
    cmp_end = lax.broadcasted_iota(jnp.int32, (ncp, 1), 0) * CMP_STRIDE + (cmp_blk - 1)
    mask_c = cmp_end <= t_row
    maskf_c = mask_c.astype(F32)
    p_c = [_masked_softmax_t(s_c[:, h], mask_c, maskf_c) for h in heads]
    o_c = _dot(vct_ref[0], jnp.concatenate(p_c, axis=1).astype(BF16))

    p_sum = p_c[0]
    for r in range(1, R):
        p_sum = p_sum + p_c[r]
    imp = _dot(wselt_ref[...], p_sum.astype(BF16))
    ns = imp.shape[0]
    jblk = lax.broadcasted_iota(jnp.int32, (ns, 1), 0)
    cur = t_row // SEL_BLK
    forced = (jblk == 0) | (jblk == cur) | (jblk == cur - 1)
    st = jnp.where(forced, BIG, imp)
    st = jnp.where(jblk <= cur, st, -BIG)
    jrow = lax.broadcasted_iota(jnp.int32, (ns, T), 0).astype(F32)
    sel = jnp.zeros((ns, T), F32)
    for _ in range(min(N_SEL, ns)):
        top = jnp.max(st, axis=0, keepdims=True)
        first = jnp.min(jnp.where(st == top, jrow, float(ns)), axis=0, keepdims=True)
        hit = jrow == first
        sel = jnp.where(hit, 1.0, sel)
        st = jnp.where(hit, NEG, st)
    sel = sel.astype(BF16)

    tk = min(ATT_KV_TILE, S)
    hg = min(ATT_HEAD_GROUP, R)
    groups = [slice(g0 * T, (g0 + hg) * T) for g0 in range(0, R, hg)]
    qt_g = [qt[:, g] for g in groups]

    def kv_step(kt, carry):
        k0 = pl.multiple_of(kt * tk, tk)
        k_tile = ks_ref[0, pl.ds(k0, tk), :]
        vt_tile = vst_ref[0, :, pl.ds(k0, tk)]
        chosen = _dot(et_ref[pl.ds(k0, tk), :], sel)
        kpos = k0 + lax.broadcasted_iota(jnp.int32, (tk, 1), 0)
        bias = jnp.where((chosen > 0.5) & (kpos <= t_row), 0.0, NEG)
        out = []
        for (m, l, acc), q_g in zip(carry, qt_g):
            s = _dot(k_tile, q_g)
            m_new, p = [], []
            for r in range(hg):
                h = slice(r * T, (r + 1) * T)
                s_r = s[:, h] + bias
                m_r = jnp.maximum(m[:, h], jnp.max(s_r, axis=0, keepdims=True))
                p.append(jnp.exp2(s_r - m_r))
                m_new.append(m_r)
            m_new = jnp.concatenate(m_new, axis=1)
            alpha = jnp.exp2(m - m_new)
            l_new = alpha * l + jnp.concatenate([jnp.sum(p_r, axis=0, keepdims=True) for p_r in p], axis=1)
            pt = jnp.concatenate([p_r.astype(BF16) for p_r in p], axis=1)
            out.append((m_new, l_new, alpha * acc + _dot(vt_tile, pt)))
        return tuple(out)

    n_tiles = (q0 + T + tk - 1) // tk
    init = tuple((jnp.full((1, hg * T), NEG, F32), jnp.zeros((1, hg * T), F32), jnp.zeros((dh, hg * T), F32))
                 for _ in groups)
    fin = lax.fori_loop(0, n_tiles, kv_step, init)
    o_s = jnp.concatenate([acc * (1.0 / jnp.maximum(l, TINY)) for _, l, acc in fin], axis=1)

    nw = min(WINDOW + T, S)
    w0 = pl.multiple_of(jnp.clip(q0 - WINDOW, 0, S - nw), T)
    s_w = _dot(kw_ref[0, pl.ds(w0, nw), :], qt)
    diff = t_row - (w0 + lax.broadcasted_iota(jnp.int32, (nw, 1), 0))
    bias_w = jnp.where((diff >= 0) & (diff < WINDOW), 0.0, NEG)
    p_w, inv_w = [], []
    for h in heads:
        s_r = s_w[:, h] + bias_w
        p_r = jnp.exp2(s_r - jnp.max(s_r, axis=0, keepdims=True))
        inv_w.append(1.0 / jnp.maximum(jnp.sum(p_r, axis=0, keepdims=True), TINY))
        p_w.append(p_r.astype(BF16))
    o_w = _dot(vwt_ref[0, :, pl.ds(w0, nw)], jnp.concatenate(p_w, axis=1)) * jnp.concatenate(inv_w, axis=1)

    gt = gate_ref[0].T
    out = []
    for r, h in enumerate(heads):
        c0 = N_BRANCH * r
        o_r = gt[c0:c0 + 1] * o_c[:, h] + gt[c0 + 1:c0 + 2] * o_s[:, h] + gt[c0 + 2:c0 + 3] * o_w[:, h]
        out.append(o_r.T)
    o_ref[0] = jnp.concatenate(out, axis=1).astype(o_ref.dtype)


def _nsa_attention(q, gates, cmp_kv, kvh, wselt, et, B, G, R, S, dh, cmp_blk):
    T = _tile(S, ATT_Q_TILE)
    ns, ncp = wselt.shape
    vct = jnp.swapaxes(cmp_kv[1], -1, -2).reshape(B * G, dh, ncp)
    vst = jnp.swapaxes(kvh[3 * G:4 * G], -1, -2)
    vwt = jnp.swapaxes(kvh[5 * G:6 * G], -1, -2)

    def k_spec(br):
        return pl.BlockSpec((1, S, dh), lambda b, g, i: (br * G + g, b, 0))

    vt_spec = pl.BlockSpec((1, dh, S), lambda b, g, i: (g, 0, b))

    return pl.pallas_call(
        functools.partial(_nsa_kernel, T=T, R=R, dh=dh, S=S, cmp_blk=cmp_blk),
        grid=(B, G, S // T),
        in_specs=[
            pl.BlockSpec((1, T, R * dh), lambda b, g, i: (b, i, g)),
            pl.BlockSpec((1, T, LANES), lambda b, g, i: (b, i, g)),
            pl.BlockSpec((1, 1, 1, ncp, dh), lambda b, g, i: (0, b, g, 0, 0)),
            pl.BlockSpec((1, dh, ncp), lambda b, g, i: (b * G + g, 0, 0)),
            k_spec(2), vt_spec, k_spec(4), vt_spec,
            pl.BlockSpec((ns, ncp), lambda b, g, i: (0, 0)),
            pl.BlockSpec((S, ns), lambda b, g, i: (0, 0)),
        ],
        out_specs=pl.BlockSpec((1, T, R * dh), lambda b, g, i: (b, i, g)),
        out_shape=jax.ShapeDtypeStruct((B, S, G * R * dh), BF16),
        compiler_params=_params("parallel", "parallel", "arbitrary"),
        name="nsa_attention",
    )(q, gates, cmp_kv, vct, kvh, vst, kvh, vwt, wselt, et)


def _selection_constants(S, ncp, cmp_blk):
    ns = S // SEL_BLK
    nc = (S - cmp_blk) // CMP_STRIDE + 1
    cs = jnp.arange(ncp) * CMP_STRIDE
    ss = jnp.arange(ns) * SEL_BLK
    ov = jnp.clip(jnp.minimum(cs[None, :] + cmp_blk, ss[:, None] + SEL_BLK) - jnp.maximum(cs[None, :], ss[:, None]),
                  0, None)
    wselt = jnp.where(jnp.arange(ncp)[None, :] < nc, ov / CMP_STRIDE, 0.0).astype(BF16)
    et = (jnp.arange(S)[:, None] // SEL_BLK == jnp.arange(ns)[None, :]).astype(BF16)
    return wselt, et


def kernel(x, c, w_ada, b_ada, ada_table, norm_mix_g, norm_ffn_g, w_ffn1, w_ffn2,
           conv_w_pw1, conv_b_pw1, conv_w_dw, conv_b_dw, conv_ln_g, conv_ln_b, conv_w_pw2, conv_b_pw2,
           kv_norm_g, w_kv, k_norm_g, cmp_pos, cmp_w1, cmp_b1, cmp_w2, cmp_b2,
           attn_w_qg, attn_q_norm_g, attn_w_o):
    B, S, D = x.shape
    M = B * S
    depth = ada_table.shape[0]
    n_a = conv_w_pw1.shape[0]
    dh = k_norm_g.shape[1]
    G = w_kv.shape[1] // (2 * N_BRANCH * dh)
    HD = attn_w_o.shape[1]
    R = HD // (G * dh)
    cmp_blk = cmp_pos.shape[1]
    assert R * N_BRANCH <= LANES and S % SEL_BLK == 0 and S % CMP_STRIDE == 0

    mod = _ada_mod(c, w_ada, b_ada, ada_table).reshape(depth, B, N_MOD, D)
    x2 = x.reshape(M, D)
    zeros_bd = jnp.zeros((B, D), F32)
    kvh = cmp_kv = None
    wselt, et = _selection_constants(S, S // CMP_STRIDE, cmp_blk)

    for l in range(depth):
        sh_m, sc_m, g_m, sh_f, sc_f, g_f = (mod[l, :, i] for i in range(N_MOD))
        h = _norm_mod(x2.reshape(B, S, D), norm_mix_g[l], sc_m, sh_m)
        if l < n_a:
            u = _mm_glu(h.reshape(M, D), conv_w_pw1[l].astype(BF16), conv_b_pw1[l])
            v = _dwconv_ln_silu(u.reshape(B, S, D), conv_w_dw[l], conv_b_dw[l], conv_ln_g[l], conv_ln_b[l])
            x2 = _mm_resid(v.reshape(M, D), conv_w_pw2[l].astype(BF16), conv_b_pw2[l], x2, g_m, S)
        else:
            i = l - n_a
            hm = h.reshape(M, D)
            q = _mm_q(hm, attn_w_qg[i][:, :HD].astype(BF16), attn_q_norm_g[i], dh)
            wg = attn_w_qg[i][:, HD:].reshape(D, G, R * N_BRANCH)
            wg = jnp.pad(wg, ((0, 0), (0, 0), (0, LANES - R * N_BRANCH))).reshape(D, G * LANES)
            gates = _mm_gate(hm, wg.astype(BF16))
            o = _nsa_attention(q.reshape(B, S, HD), gates.reshape(B, S, G * LANES), cmp_kv, kvh, wselt, et,
                               B, G, R, S, dh, cmp_blk)
            x2 = _mm_resid(o.reshape(M, HD), attn_w_o[i].astype(BF16), jnp.zeros((D,), F32), x2, g_m, S)
        h = _norm_mod(x2.reshape(B, S, D), norm_ffn_g[l], sc_f, sh_f)
        hid = _mm_relu2(h.reshape(M, D), w_ffn1[l].astype(BF16))
        x2 = _mm_kacc_resid(hid, w_ffn2[l].astype(BF16), x2, g_f, S)
        if l == n_a - 1:
            hk = _norm_mod(x2.reshape(B, S, D), kv_norm_g, zeros_bd, zeros_bd)
            ones = jnp.ones((dh,), F32)
            gain = jnp.stack([ones, ones, k_norm_g[1], ones, k_norm_g[2], ones]).reshape(2 * N_BRANCH, 1, dh)
            flag = jnp.array([0.0, 0.0, 1.0, 0.0, 1.0, 0.0], F32)[:, None, None] * jnp.ones((1, 1, dh), F32)
            kvh = _mm_kv(hk.reshape(M, D), w_kv.astype(BF16), gain, flag, G, dh)
            cmp_kv = _compress(kvh, cmp_pos, cmp_w1, cmp_b1, cmp_w2, cmp_b2, k_norm_g[0], B, G, S, dh)
    return x2.reshape(B, S, D)
```

```python
import functools

import jax
import jax.numpy as jnp
from jax import lax
from jax.experimental import pallas as pl
from jax.experimental.pallas import tpu as pltpu

F32 = jnp.float32
BF16 = jnp.bfloat16

CMP_STRIDE = 16
SEL_BLK = 64
N_SEL = 16
WINDOW = 512
N_BRANCH = 3
N_MOD = 6
EPS = 1e-6
NEG = -1e30
BIG = 1e9
TINY = 1e-20
LOG2E = 1.4426950408889634

LANES = 128
SUBLANES = 8
VMEM_LIMIT_BYTES = 56 * 1024 * 1024

ATT_Q_TILE = 256
ATT_KV_TILE = 1024
CONV_HALO = 32


def _tile(dim, pref):
    t = min(dim, pref)
    assert dim % t == 0, (dim, pref)
    return t


def _params(*sem):
    return pltpu.CompilerParams(dimension_semantics=sem, vmem_limit_bytes=VMEM_LIMIT_BYTES)


def _dot(a, b):
    return jnp.dot(a, b, preferred_element_type=F32)


def _sigmoid(x):
    return 1.0 / (1.0 + jnp.exp(-x))


def _ada_kernel(c_ref, w_ref, b_ref, tab_ref, o_ref):
    c = c_ref[...]
    a = (c * _sigmoid(c)).astype(BF16)
    base = _dot(a, w_ref[...].astype(BF16)) + b_ref[...]
    for l in range(o_ref.shape[0]):
        o_ref[l] = base + tab_ref[l]


def _ada_mod(c, w_ada, b_ada, ada_table):
    B, D = c.shape
    depth = ada_table.shape[0]
    N = w_ada.shape[1]
    bn = _tile(N, 1024)
    return pl.pallas_call(
        _ada_kernel,
        grid=(N // bn,),
        in_specs=[
            pl.BlockSpec((B, D), lambda j: (0, 0)),
            pl.BlockSpec((D, bn), lambda j: (0, j)),
            pl.BlockSpec((1, bn), lambda j: (0, j)),
            pl.BlockSpec((depth, 1, bn), lambda j: (0, 0, j)),
        ],
        out_specs=pl.BlockSpec((depth, B, bn), lambda j: (0, 0, j)),
        out_shape=jax.ShapeDtypeStruct((depth, B, N), F32),
        compiler_params=_params("parallel"),
        name="ada_mod",
    )(c, w_ada, b_ada.reshape(1, N), ada_table.reshape(depth, 1, N))


def _norm_mod_kernel(x_ref, g_ref, sc_ref, sh_ref, o_ref):
    x = x_ref[0]
    y = x * lax.rsqrt(jnp.mean(x * x, axis=-1, keepdims=True) + EPS)
    o_ref[0] = ((y * g_ref[...]) * (1.0 + sc_ref[0]) + sh_ref[0]).astype(o_ref.dtype)


def _norm_mod(x, g, scale, shift):
    B, S, D = x.shape
    tm = _tile(S, 256)
    return pl.pallas_call(
        _norm_mod_kernel,
        grid=(B, S // tm),
        in_specs=[
            pl.BlockSpec((1, tm, D), lambda b, i: (b, i, 0)),
            pl.BlockSpec((1, D), lambda b, i: (0, 0)),
            pl.BlockSpec((1, 1, D), lambda b, i: (b, 0, 0)),
            pl.BlockSpec((1, 1, D), lambda b, i: (b, 0, 0)),
        ],
        out_specs=pl.BlockSpec((1, tm, D), lambda b, i: (b, i, 0)),
        out_shape=jax.ShapeDtypeStruct((B, S, D), BF16),
        compiler_params=_params("parallel", "parallel"),
        name="norm_mod",
    )(x, g.reshape(1, D), scale.reshape(B, 1, D), shift.reshape(B, 1, D))


def _mm_glu_kernel(h_ref, wa_ref, wg_ref, ba_ref, bg_ref, o_ref):
    h = h_ref[...]
    a = _dot(h, wa_ref[0]) + ba_ref[...]
    gt = _dot(h, wg_ref[0]) + bg_ref[...]
    o_ref[...] = a * _sigmoid(gt)


def _mm_glu(h, w, layer, b):
    M, K = h.shape
    N = w.shape[2] // 2
    bm, bn = _tile(M, 1024), _tile(N, 512)
    nj = N // bn
    b2 = b.reshape(1, 2 * N)
    return pl.pallas_call(
        _mm_glu_kernel,
        grid=(M // bm, nj),
        in_specs=[
            pl.BlockSpec((bm, K), lambda i, j: (i, 0)),
            pl.BlockSpec((1, K, bn), lambda i, j: (layer, 0, j)),
            pl.BlockSpec((1, K, bn), lambda i, j: (layer, 0, j + nj)),
            pl.BlockSpec((1, bn), lambda i, j: (0, j)),
            pl.BlockSpec((1, bn), lambda i, j: (0, j + nj)),
        ],
        out_specs=pl.BlockSpec((bm, bn), lambda i, j: (i, j)),
        out_shape=jax.ShapeDtypeStruct((M, N), F32),
        compiler_params=_params("parallel", "parallel"),
        name="mm_glu",
    )(h, w, w, b2, b2)


def _mm_resid_kernel(h_ref, w_ref, b_ref, x_ref, g_ref, o_ref):
    y = _dot(h_ref[...], w_ref[0]) + b_ref[...]
    o_ref[...] = x_ref[...] + g_ref[0] * y


def _mm_resid(h, w, layer, bias, xres, gate, rows_per_batch):
    M, K = h.shape
    N = w.shape[2]
    B = gate.shape[0]
    bm, bn = _tile(rows_per_batch, 1024), _tile(N, 1024)
    nb = rows_per_batch // bm
    return pl.pallas_call(
        _mm_resid_kernel,
        grid=(M // bm, N // bn),
        in_specs=[
            pl.BlockSpec((bm, K), lambda i, j: (i, 0)),
            pl.BlockSpec((1, K, bn), lambda i, j: (layer, 0, j)),
            pl.BlockSpec((1, bn), lambda i, j: (0, j)),
            pl.BlockSpec((bm, bn), lambda i, j: (i, j)),
            pl.BlockSpec((1, 1, bn), lambda i, j: (i // nb, 0, j)),
        ],
        out_specs=pl.BlockSpec((bm, bn), lambda i, j: (i, j)),
        out_shape=jax.ShapeDtypeStruct((M, N), F32),
        compiler_params=_params("parallel", "parallel"),
        name="mm_resid",
    )(h, w, bias.reshape(1, N), xres, gate.reshape(B, 1, N))


def _mm_relu2_kernel(h_ref, w_ref, o_ref):
    y = jnp.maximum(_dot(h_ref[...], w_ref[0].astype(BF16)), 0.0)
    o_ref[...] = (y * y).astype(o_ref.dtype)


def _mm_relu2(h, w, layer):
    M, K = h.shape
    N = w.shape[2]
    bm, bn = _tile(M, 1024), _tile(N, 512)
    return pl.pallas_call(
        _mm_relu2_kernel,
        grid=(M // bm, N // bn),
        in_specs=[
            pl.BlockSpec((bm, K), lambda i, j: (i, 0)),
            pl.BlockSpec((1, K, bn), lambda i, j: (layer, 0, j)),
        ],
        out_specs=pl.BlockSpec((bm, bn), lambda i, j: (i, j)),
        out_shape=jax.ShapeDtypeStruct((M, N), BF16),
        compiler_params=_params("parallel", "parallel"),
        name="mm_relu2",
    )(h, w)


def _mm_kacc_resid_kernel(h_ref, w_ref, x_ref, g_ref, o_ref, acc_ref):
    k = pl.program_id(2)

    @pl.when(k == 0)
    def _():
        acc_ref[...] = jnp.zeros_like(acc_ref)

    acc_ref[...] += _dot(h_ref[...], w_ref[0])

    @pl.when(k == pl.num_programs(2) - 1)
    def _():
        o_ref[...] = x_ref[...] + g_ref[0] * acc_ref[...]


def _mm_kacc_resid(h, w, layer, xres, gate, rows_per_batch):
    M, K = h.shape
    N = w.shape[2]
    B = gate.shape[0]
    bm, bn, bk = _tile(rows_per_batch, 1024), _tile(N, 1024), _tile(K, 2048)
    nb = rows_per_batch // bm
    return pl.pallas_call(
        _mm_kacc_resid_kernel,
        grid=(M // bm, N // bn, K // bk),
        in_specs=[
            pl.BlockSpec((bm, bk), lambda i, j, k: (i, k)),
            pl.BlockSpec((1, bk, bn), lambda i, j, k: (layer, k, j)),
            pl.BlockSpec((bm, bn), lambda i, j, k: (i, j)),
            pl.BlockSpec((1, 1, bn), lambda i, j, k: (i // nb, 0, j)),
        ],
        out_specs=pl.BlockSpec((bm, bn), lambda i, j, k: (i, j)),
        out_shape=jax.ShapeDtypeStruct((M, N), F32),
        scratch_shapes=[pltpu.VMEM((bm, bn), F32)],
        compiler_params=_params("parallel", "parallel", "arbitrary"),
        name="mm_kacc_resid",
    )(h, w, xres, gate.reshape(B, 1, N))


def _mm_q_kernel(h_ref, w_ref, g_ref, o_ref, *, dh, scale):
    acc = _dot(h_ref[...], w_ref[0])
    g = g_ref[...] * scale
    for c in range(acc.shape[1] // dh):
        blk = acc[:, c * dh:(c + 1) * dh]
        r = lax.rsqrt(jnp.mean(blk * blk, axis=-1, keepdims=True) + EPS)
        o_ref[:, c * dh:(c + 1) * dh] = (blk * r * g).astype(o_ref.dtype)


def _mm_q(h, w, layer, N, q_gain, dh):
    M, K = h.shape
    bm, bn = _tile(M, 1024), _tile(N, 1024)
    return pl.pallas_call(
        functools.partial(_mm_q_kernel, dh=dh, scale=float(dh) ** -0.5 * LOG2E),
        grid=(M // bm, N // bn),
        in_specs=[
            pl.BlockSpec((bm, K), lambda i, j: (i, 0)),
            pl.BlockSpec((1, K, bn), lambda i, j: (layer, 0, j)),
            pl.BlockSpec((1, dh), lambda i, j: (0, 0)),
        ],
        out_specs=pl.BlockSpec((bm, bn), lambda i, j: (i, j)),
        out_shape=jax.ShapeDtypeStruct((M, N), BF16),
        compiler_params=_params("parallel", "parallel"),
        name="mm_q",
    )(h, w, q_gain.reshape(1, dh))


def _mm_gate_kernel(h_ref, w_ref, o_ref):
    o_ref[...] = _sigmoid(_dot(h_ref[...], w_ref[...]))


def _mm_gate(h, w):
    M, K = h.shape
    N = w.shape[1]
    bm = _tile(M, 1024)
    return pl.pallas_call(
        _mm_gate_kernel,
        grid=(M // bm,),
        in_specs=[
            pl.BlockSpec((bm, K), lambda i: (i, 0)),
            pl.BlockSpec((K, N), lambda i: (0, 0)),
        ],
        out_specs=pl.BlockSpec((bm, N), lambda i: (i, 0)),
        out_shape=jax.ShapeDtypeStruct((M, N), F32),
        compiler_params=_params("parallel"),
        name="mm_gate",
    )(h, w)


def _mm_kv_kernel(h_ref, w_ref, gain_ref, flag_ref, o_ref, *, dh):
    acc = _dot(h_ref[...], w_ref[...])
    gain = gain_ref[0]
    use_norm = flag_ref[0] > 0.5
    for c in range(o_ref.shape[0]):
        blk = acc[:, c * dh:(c + 1) * dh]
        r = lax.rsqrt(jnp.mean(blk * blk, axis=-1, keepdims=True) + EPS)
        o_ref[c] = jnp.where(use_norm, blk * r * gain, blk).astype(o_ref.dtype)


def _mm_kv(h, w, gain, flag, G, dh):
    M, K = h.shape
    nbr = w.shape[1] // (G * dh)
    bm, bn = _tile(M, 1024), G * dh
    return pl.pallas_call(
        functools.partial(_mm_kv_kernel, dh=dh),
        grid=(M // bm, nbr),
        in_specs=[
            pl.BlockSpec((bm, K), lambda i, j: (i, 0)),
            pl.BlockSpec((K, bn), lambda i, j: (0, j)),
            pl.BlockSpec((1, 1, dh), lambda i, j: (j, 0, 0)),
            pl.BlockSpec((1, 1, dh), lambda i, j: (j, 0, 0)),
        ],
        out_specs=pl.BlockSpec((G, bm, dh), lambda i, j: (j, i, 0)),
        out_shape=jax.ShapeDtypeStruct((nbr * G, M, dh), BF16),
        compiler_params=_params("parallel", "parallel"),
        name="mm_kv",
    )(h, w, gain, flag)


def _dwconv_kernel(ucur_ref, uprev_ref, w_ref, bdw_ref, lng_ref, lnb_ref, o_ref, ext_ref, y_ref, *, tt, kc, D):
    i = pl.program_id(1)
    n = tt + CONV_HALO
    ext_ref[0:CONV_HALO, :] = jnp.where(i > 0, uprev_ref[0], 0.0)
    ext_ref[CONV_HALO:n, :] = ucur_ref[0]
    first = CONV_HALO - (kc - 1)
    nchunk = D // LANES

    def conv_chunk(c, carry):
        s1, s2 = carry
        col = pl.ds(pl.multiple_of(c * LANES, LANES), LANES)
        big = ext_ref[:, col]
        shifted = [big] + [pltpu.roll(big, n - s, axis=0) for s in range(1, SUBLANES)]
        acc = jnp.zeros((tt, LANES), F32)
        for k in range(kc):
            a, s = divmod(first + k, SUBLANES)
            acc = acc + w_ref[k:k + 1, col] * shifted[s][a * SUBLANES:a * SUBLANES + tt]
        y = acc + bdw_ref[:, col]
        y_ref[:, col] = y
        return s1 + y, s2 + y * y

    zero = jnp.zeros((tt, LANES), F32)
    s1, s2 = lax.fori_loop(0, nchunk, conv_chunk, (zero, zero))
    mean = jnp.sum(s1, axis=-1, keepdims=True) * (1.0 / D)
    var = jnp.maximum(jnp.sum(s2, axis=-1, keepdims=True) * (1.0 / D) - mean * mean, 0.0)
    rstd = lax.rsqrt(var + EPS)

    def norm_chunk(c, carry):
        col = pl.ds(pl.multiple_of(c * LANES, LANES), LANES)
        yn = (y_ref[:, col] - mean) * rstd * lng_ref[:, col] + lnb_ref[:, col]
        o_ref[0, :, col] = (yn * _sigmoid(yn)).astype(o_ref.dtype)
        return carry

    lax.fori_loop(0, nchunk, norm_chunk, 0)


def _dwconv_ln_silu(u, w_dw, b_dw, ln_g, ln_b):
    B, S, D = u.shape
    kc = w_dw.shape[0]
    assert kc - 1 <= CONV_HALO and D % LANES == 0
    tt = _tile(S, 128)
    assert tt % CONV_HALO == 0
    hb = tt // CONV_HALO
    return pl.pallas_call(
        functools.partial(_dwconv_kernel, tt=tt, kc=kc, D=D),
        grid=(B, S // tt),
        in_specs=[
            pl.BlockSpec((1, tt, D), lambda b, i: (b, i, 0)),
            pl.BlockSpec((1, CONV_HALO, D), lambda b, i: (b, jnp.maximum(i * hb - 1, 0), 0)),
            pl.BlockSpec((kc, D), lambda b, i: (0, 0)),
            pl.BlockSpec((1, D), lambda b, i: (0, 0)),
            pl.BlockSpec((1, D), lambda b, i: (0, 0)),
            pl.BlockSpec((1, D), lambda b, i: (0, 0)),
        ],
        out_specs=pl.BlockSpec((1, tt, D), lambda b, i: (b, i, 0)),
        out_shape=jax.ShapeDtypeStruct((B, S, D), BF16),
        scratch_shapes=[pltpu.VMEM((tt + CONV_HALO, D), F32), pltpu.VMEM((tt, D), F32)],
        compiler_params=_params("parallel", "parallel"),
        name="dwconv_ln_silu",
    )(u, u, w_dw, b_dw.reshape(1, D), ln_g.reshape(1, D), ln_b.reshape(1, D))


def _compress_kernel(t_ref, pos_ref, w1a_ref, w1b_ref, b1_ref, w2_ref, b2_ref, g_ref, o_ref):
    br = pl.program_id(0)
    t = t_ref[0].astype(F32)
    n = t.shape[0]
    pos = pos_ref[0]
    a = _dot((t + pos[0:1]).astype(BF16), w1a_ref[0])
    b = _dot((t + pos[1:2]).astype(BF16), w1b_ref[0])
    hid = a + pltpu.roll(b, n - 1, axis=0) + b1_ref[0]
    hid = hid * _sigmoid(hid)
    out = _dot(hid.astype(BF16), w2_ref[0]) + b2_ref[0]
    normed = out * lax.rsqrt(jnp.mean(out * out, axis=-1, keepdims=True) + EPS) * g_ref[...]
    out = jnp.where(br == 0, normed, out)
    row = lax.broadcasted_iota(jnp.int32, out.shape, 0)
    o_ref[0, 0, 0] = jnp.where(row < n - 1, out, 0.0).astype(o_ref.dtype)


def _compress(kvh, cmp_pos, cmp_w1, cmp_b1, cmp_w2, cmp_b2, k_gain0, B, G, S, dh):
    blk = cmp_pos.shape[1]
    assert blk == 2 * CMP_STRIDE
    half = CMP_STRIDE * dh
    n = S // CMP_STRIDE
    hid = cmp_w1.shape[2]
    t = kvh[:2 * G].reshape(2 * G, B * n, half)
    pos = cmp_pos.reshape(2, 2, half)
    w1 = cmp_w1.astype(BF16)
    return pl.pallas_call(
        _compress_kernel,
        grid=(2, B, G),
        in_specs=[
            pl.BlockSpec((1, n, half), lambda i, b, g: (i * G + g, b, 0)),
            pl.BlockSpec((1, 2, half), lambda i, b, g: (i, 0, 0)),
            pl.BlockSpec((1, half, hid), lambda i, b, g: (i, 0, 0)),
            pl.BlockSpec((1, half, hid), lambda i, b, g: (i, 1, 0)),
            pl.BlockSpec((1, 1, hid), lambda i, b, g: (i, 0, 0)),
            pl.BlockSpec((1, hid, dh), lambda i, b, g: (i, 0, 0)),
            pl.BlockSpec((1, 1, dh), lambda i, b, g: (i, 0, 0)),
            pl.BlockSpec((1, dh), lambda i, b, g: (0, 0)),
        ],
        out_specs=pl.BlockSpec((1, 1, 1, n, dh), lambda i, b, g: (i, b, g, 0, 0)),
        out_shape=jax.ShapeDtypeStruct((2, B, G, n, dh), BF16),
        compiler_params=_params("parallel", "parallel", "parallel"),
        name="compress_kv",
    )(t, pos, w1, w1, cmp_b1.reshape(2, 1, hid), cmp_w2.astype(BF16), cmp_b2.reshape(2, 1, dh),
      k_gain0.reshape(1, dh))


def _masked_softmax_t(s, bias, any_valid):
    s = s + bias
    p = jnp.exp2(s - jnp.max(s, axis=0, keepdims=True))
    inv = jnp.where(any_valid, 1.0 / jnp.maximum(jnp.sum(p, axis=0, keepdims=True), TINY), 0.0)
    return p * inv


def _nsa_kernel(q_ref, gate_ref, kc_ref, vct_ref, ks_ref, vst_ref, kw_ref, vwt_ref, wselt_ref, et_ref, o_ref,
                *, T, R, dh, S, cmp_blk):
    q0 = pl.program_id(2) * T
    q = q_ref[0].astype(F32)
    qt = jnp.concatenate([q[:, r * dh:(r + 1) * dh].T for r in range(R)], axis=1).astype(BF16)
    t_row = q0 + lax.broadcasted_iota(jnp.int32, (1, T), 1)
    heads = [slice(r * T, (r + 1) * T) for r in range(R)]

    kc = kc_ref[0, 0, 0]
    ncp = kc.shape[0]
    s_c = _dot(kc, qt)
    cmp_end = lax.broadcasted_iota(jnp.int32, (ncp, 1), 0) * CMP_STRIDE + (cmp_blk - 1)
    bias_c = jnp.where(cmp_end <= t_row, 0.0, NEG)
    any_c = t_row >= cmp_blk - 1
    p_c = [_masked_softmax_t(s_c[:, h], bias_c, any_c) for h in heads]
    o_c = _dot(vct_ref[0], jnp.concatenate(p_c, axis=1).astype(BF16))

    nw = min(WINDOW + T, S)
    w0 = pl.multiple_of(jnp.clip(q0 - WINDOW, 0, S - nw), T)
    s_w = _dot(kw_ref[0, pl.ds(w0, nw), :], qt)
    diff = t_row - (w0 + lax.broadcasted_iota(jnp.int32, (nw, 1), 0))
    bias_w = jnp.where((diff >= 0) & (diff < WINDOW), 0.0, NEG)
    p_w, inv_w = [], []
    for h in heads:
        s_r = s_w[:, h] + bias_w
        p_r = jnp.exp2(s_r - jnp.max(s_r, axis=0, keepdims=True))
        inv_w.append(1.0 / jnp.maximum(jnp.sum(p_r, axis=0, keepdims=True), TINY))
        p_w.append(p_r.astype(BF16))
    o_w = _dot(vwt_ref[0, :, pl.ds(w0, nw)], jnp.concatenate(p_w, axis=1)) * jnp.concatenate(inv_w, axis=1)

    p_sum = p_c[0]
    for r in range(1, R):
        p_sum = p_sum + p_c[r]
    imp = _dot(wselt_ref[...], p_sum.astype(BF16))
    ns = imp.shape[0]
    jblk = lax.broadcasted_iota(jnp.int32, (ns, 1), 0)
    cur = t_row // SEL_BLK
    forced = (jblk == 0) | (jblk == cur) | (jblk == cur - 1)
    st = jnp.where(forced, BIG, imp)
    st = jnp.where(jblk <= cur, st, -BIG)
    jrow = lax.broadcasted_iota(jnp.int32, (ns, T), 0).astype(F32)
    sel = jnp.zeros((ns, T), F32)
    for _ in range(min(N_SEL, ns)):
        top = jnp.max(st, axis=0, keepdims=True)
        first = jnp.min(jnp.where(st == top, jrow, float(ns)), axis=0, keepdims=True)
        hit = jrow == first
        sel = jnp.where(hit, 1.0, sel)
        st = jnp.where(hit, NEG, st)
    chosen = _dot(et_ref[pl.ds(q0, T), :], sel.astype(BF16))
    kpos = q0 + lax.broadcasted_iota(jnp.int32, (T, 1), 0)
    bias_d = jnp.where((chosen > 0.5) & (kpos <= t_row), 0.0, NEG)
    s_d = _dot(ks_ref[0, pl.ds(q0, T), :], qt)
    m0, l0, p0 = [], [], []
    for h in heads:
        s_r = s_d[:, h] + bias_d
        m_r = jnp.max(s_r, axis=0, keepdims=True)
        p_r = jnp.exp2(s_r - m_r)
        m0.append(m_r)
        l0.append(jnp.sum(p_r, axis=0, keepdims=True))
        p0.append(p_r.astype(BF16))
    init = (jnp.concatenate(m0, axis=1), jnp.concatenate(l0, axis=1),
            _dot(vst_ref[0, :, pl.ds(q0, T)], jnp.concatenate(p0, axis=1)))

    before = jblk * SEL_BLK < q0
    sel_bias = jnp.where((sel > 0.5) & before, 0.0, NEG).astype(BF16)
    qx = jnp.concatenate([qt, jnp.concatenate([sel_bias] * R, axis=1)], axis=0)
    tk = min(ATT_KV_TILE, S)

    def kv_step(kt, carry):
        m, l, acc = carry
        k0 = pl.multiple_of(kt * tk, tk)
        kx = jnp.concatenate([ks_ref[0, pl.ds(k0, tk), :], et_ref[pl.ds(k0, tk), :]], axis=1)
        s = _dot(kx, qx)
        m_new, p = [], []
        for h in heads:
            m_r = jnp.maximum(m[:, h], jnp.max(s[:, h], axis=0, keepdims=True))
            p.append(jnp.exp2(s[:, h] - m_r))
            m_new.append(m_r)
        m_new = jnp.concatenate(m_new, axis=1)
        alpha = jnp.exp2(m - m_new)
        l_new = alpha * l + jnp.concatenate([jnp.sum(p_r, axis=0, keepdims=True) for p_r in p], axis=1)
        pt = jnp.concatenate([p_r.astype(BF16) for p_r in p], axis=1)
        return m_new, l_new, alpha * acc + _dot(vst_ref[0, :, pl.ds(k0, tk)], pt)

    _, l_s, acc_s = lax.fori_loop(0, (q0 + tk - 1) // tk, kv_step, init)
    o_s = acc_s * (1.0 / jnp.maximum(l_s, TINY))

    gt = gate_ref[0].T
    out = []
    for r, h in enumerate(heads):
        c0 = N_BRANCH * r
        o_r = gt[c0:c0 + 1] * o_c[:, h] + gt[c0 + 1:c0 + 2] * o_s[:, h] + gt[c0 + 2:c0 + 3] * o_w[:, h]
        out.append(o_r.T)
    o_ref[0] = jnp.concatenate(out, axis=1).astype(o_ref.dtype)


def _nsa_attention(q, gates, cmp_kv, kvh, wselt, et, B, G, R, S, dh, cmp_blk):
    T = _tile(S, ATT_Q_TILE)
    ns, ncp = wselt.shape
    vct = jnp.swapaxes(cmp_kv[1], -1, -2).reshape(B * G, dh, ncp)
    vst = jnp.swapaxes(kvh[3 * G:4 * G], -1, -2)
    vwt = jnp.swapaxes(kvh[5 * G:6 * G], -1, -2)

    def k_spec(br):
        return pl.BlockSpec((1, S, dh), lambda b, g, i: (br * G + g, b, 0))

    vt_spec = pl.BlockSpec((1, dh, S), lambda b, g, i: (g, 0, b))

    return pl.pallas_call(
        functools.partial(_nsa_kernel, T=T, R=R, dh=dh, S=S, cmp_blk=cmp_blk),
        grid=(B, G, S // T),
        in_specs=[
            pl.BlockSpec((1, T, R * dh), lambda b, g, i: (b, i, g)),
            pl.BlockSpec((1, T, LANES), lambda b, g, i: (b, i, g)),
            pl.BlockSpec((1, 1, 1, ncp, dh), lambda b, g, i: (0, b, g, 0, 0)),
            pl.BlockSpec((1, dh, ncp), lambda b, g, i: (b * G + g, 0, 0)),
            k_spec(2), vt_spec, k_spec(4), vt_spec,
            pl.BlockSpec((ns, ncp), lambda b, g, i: (0, 0)),
            pl.BlockSpec((S, ns), lambda b, g, i: (0, 0)),
        ],
        out_specs=pl.BlockSpec((1, T, R * dh), lambda b, g, i: (b, i, g)),
        out_shape=jax.ShapeDtypeStruct((B, S, G * R * dh), BF16),
        compiler_params=_params("parallel", "parallel", "arbitrary"),
        name="nsa_attention",
    )(q, gates, cmp_kv, vct, kvh, vst, kvh, vwt, wselt, et)


def _selection_constants(S, ncp, cmp_blk):
    ns = S // SEL_BLK
    nc = (S - cmp_blk) // CMP_STRIDE + 1
    cs = jnp.arange(ncp) * CMP_STRIDE
    ss = jnp.arange(ns) * SEL_BLK
    ov = jnp.clip(jnp.minimum(cs[None, :] + cmp_blk, ss[:, None] + SEL_BLK) - jnp.maximum(cs[None, :], ss[:, None]),
                  0, None)
    wselt = jnp.where(jnp.arange(ncp)[None, :] < nc, ov / CMP_STRIDE, 0.0).astype(BF16)
    et = (jnp.arange(S)[:, None] // SEL_BLK == jnp.arange(ns)[None, :]).astype(BF16)
    return wselt, et


def kernel(x, c, w_ada, b_ada, ada_table, norm_mix_g, norm_ffn_g, w_ffn1, w_ffn2,
           conv_w_pw1, conv_b_pw1, conv_w_dw, conv_b_dw, conv_ln_g, conv_ln_b, conv_w_pw2, conv_b_pw2,
           kv_norm_g, w_kv, k_norm_g, cmp_pos, cmp_w1, cmp_b1, cmp_w2, cmp_b2,
           attn_w_qg, attn_q_norm_g, attn_w_o):
    B, S, D = x.shape
    M = B * S
    depth = ada_table.shape[0]
    n_a = conv_w_pw1.shape[0]
    dh = k_norm_g.shape[1]
    G = w_kv.shape[1] // (2 * N_BRANCH * dh)
    HD = attn_w_o.shape[1]
    R = HD // (G * dh)
    cmp_blk = cmp_pos.shape[1]
    assert R * N_BRANCH <= LANES and S % SEL_BLK == 0 and S % CMP_STRIDE == 0

    mod = _ada_mod(c, w_ada, b_ada, ada_table).reshape(depth, B, N_MOD, D)
    x2 = x.reshape(M, D)
    zeros_bd = jnp.zeros((B, D), F32)
    w_pw1, w_pw2 = conv_w_pw1.astype(BF16), conv_w_pw2.astype(BF16)
    w_qg, w_o = attn_w_qg.astype(BF16), attn_w_o.astype(BF16)
    w_ffn2_bf = w_ffn2.astype(BF16)
    kvh = cmp_kv = None
    wselt, et = _selection_constants(S, S // CMP_STRIDE, cmp_blk)

    for l in range(depth):
        sh_m, sc_m, g_m, sh_f, sc_f, g_f = (mod[l, :, i] for i in range(N_MOD))
        h = _norm_mod(x2.reshape(B, S, D), norm_mix_g[l], sc_m, sh_m)
        if l < n_a:
            u = _mm_glu(h.reshape(M, D), w_pw1, l, conv_b_pw1[l])
            v = _dwconv_ln_silu(u.reshape(B, S, D), conv_w_dw[l], conv_b_dw[l], conv_ln_g[l], conv_ln_b[l])
            x2 = _mm_resid(v.reshape(M, D), w_pw2, l, conv_b_pw2[l], x2, g_m, S)
        else:
            i = l - n_a
            hm = h.reshape(M, D)
            q = _mm_q(hm, w_qg, i, HD, attn_q_norm_g[i], dh)
            wg = attn_w_qg[i][:, HD:].reshape(D, G, R * N_BRANCH)
            wg = jnp.pad(wg, ((0, 0), (0, 0), (0, LANES - R * N_BRANCH))).reshape(D, G * LANES)
            gates = _mm_gate(hm, wg.astype(BF16))
            o = _nsa_attention(q.reshape(B, S, HD), gates.reshape(B, S, G * LANES), cmp_kv, kvh, wselt, et,
                               B, G, R, S, dh, cmp_blk)
            x2 = _mm_resid(o.reshape(M, HD), w_o, i, jnp.zeros((D,), F32), x2, g_m, S)
        h = _norm_mod(x2.reshape(B, S, D), norm_ffn_g[l], sc_f, sh_f)
        hid = _mm_relu2(h.reshape(M, D), w_ffn1, l)
        x2 = _mm_kacc_resid(hid, w_ffn2_bf, l, x2, g_f, S)
        if l == n_a - 1:
            hk = _norm_mod(x2.reshape(B, S, D), kv_norm_g, zeros_bd, zeros_bd)
            ones = jnp.ones((dh,), F32)
            gain = jnp.stack([ones, ones, k_norm_g[1], ones, k_norm_g[2], ones]).reshape(2 * N_BRANCH, 1, dh)
            flag = jnp.array([0.0, 0.0, 1.0, 0.0, 1.0, 0.0], F32)[:, None, None] * jnp.ones((1, 1, dh), F32)
            kvh = _mm_kv(hk.reshape(M, D), w_kv.astype(BF16), gain, flag, G, dh)
            cmp_kv = _compress(kvh, cmp_pos, cmp_w1, cmp_b1, cmp_w2, cmp_b2, k_norm_g[0], B, G, S, dh)
    return x2.reshape(B, S, D)
```

```python
import functools

import jax
import jax.numpy as jnp
from jax import lax
from jax.experimental import pallas as pl
from jax.experimental.pallas import tpu as pltpu

F32 = jnp.float32
BF16 = jnp.bfloat16

CMP_STRIDE = 16
SEL_BLK = 64
N_SEL = 16
WINDOW = 512
N_BRANCH = 3
N_MOD = 6
EPS = 1e-6
NEG = -1e30
BIG = 1e9
TINY = 1e-20
LOG2E = 1.4426950408889634
MAX_SCORE_SHIFT = 50.0

LANES = 128
SUBLANES = 8
VMEM_LIMIT_BYTES = 56 * 1024 * 1024

ATT_Q_TILE = 256
ATT_KV_TILE = 1024
CONV_HALO = 32


def _tile(dim, pref):
    t = min(dim, pref)
    assert dim % t == 0, (dim, pref)
    return t


def _params(*sem):
    return pltpu.CompilerParams(dimension_semantics=sem, vmem_limit_bytes=VMEM_LIMIT_BYTES)


def _dot(a, b):
    return jnp.dot(a, b, preferred_element_type=F32)


def _sigmoid(x):
    return 1.0 / (1.0 + jnp.exp(-x))


def _ada_kernel(c_ref, w_ref, b_ref, tab_ref, o_ref):
    c = c_ref[...]
    a = (c * _sigmoid(c)).astype(BF16)
    base = _dot(a, w_ref[...].astype(BF16)) + b_ref[...]
    for l in range(o_ref.shape[0]):
        o_ref[l] = base + tab_ref[l]


def _ada_mod(c, w_ada, b_ada, ada_table):
    B, D = c.shape
    depth = ada_table.shape[0]
    N = w_ada.shape[1]
    bn = _tile(N, 1024)
    return pl.pallas_call(
        _ada_kernel,
        grid=(N // bn,),
        in_specs=[
            pl.BlockSpec((B, D), lambda j: (0, 0)),
            pl.BlockSpec((D, bn), lambda j: (0, j)),
            pl.BlockSpec((1, bn), lambda j: (0, j)),
            pl.BlockSpec((depth, 1, bn), lambda j: (0, 0, j)),
        ],
        out_specs=pl.BlockSpec((depth, B, bn), lambda j: (0, 0, j)),
        out_shape=jax.ShapeDtypeStruct((depth, B, N), F32),
        compiler_params=_params("parallel"),
        name="ada_mod",
    )(c, w_ada, b_ada.reshape(1, N), ada_table.reshape(depth, 1, N))


def _norm_mod_kernel(x_ref, g_ref, sc_ref, sh_ref, o_ref):
    x = x_ref[0]
    y = x * lax.rsqrt(jnp.mean(x * x, axis=-1, keepdims=True) + EPS)
    o_ref[0] = ((y * g_ref[...]) * (1.0 + sc_ref[0]) + sh_ref[0]).astype(o_ref.dtype)


def _norm_mod(x, g, scale, shift):
    B, S, D = x.shape
    tm = _tile(S, 256)
    return pl.pallas_call(
        _norm_mod_kernel,
        grid=(B, S // tm),
        in_specs=[
            pl.BlockSpec((1, tm, D), lambda b, i: (b, i, 0)),
            pl.BlockSpec((1, D), lambda b, i: (0, 0)),
            pl.BlockSpec((1, 1, D), lambda b, i: (b, 0, 0)),
            pl.BlockSpec((1, 1, D), lambda b, i: (b, 0, 0)),
        ],
        out_specs=pl.BlockSpec((1, tm, D), lambda b, i: (b, i, 0)),
        out_shape=jax.ShapeDtypeStruct((B, S, D), BF16),
        compiler_params=_params("parallel", "parallel"),
        name="norm_mod",
    )(x, g.reshape(1, D), scale.reshape(B, 1, D), shift.reshape(B, 1, D))


def _mm_glu_kernel(h_ref, wa_ref, wg_ref, ba_ref, bg_ref, o_ref):
    h = h_ref[...]
    a = _dot(h, wa_ref[0]) + ba_ref[...]
    gt = _dot(h, wg_ref[0]) + bg_ref[...]
    o_ref[...] = a * _sigmoid(gt)


def _mm_glu(h, w, layer, b):
    M, K = h.shape
    N = w.shape[2] // 2
    bm, bn = _tile(M, 1024), _tile(N, 512)
    nj = N // bn
    b2 = b.reshape(1, 2 * N)
    return pl.pallas_call(
        _mm_glu_kernel,
        grid=(M // bm, nj),
        in_specs=[
            pl.BlockSpec((bm, K), lambda i, j: (i, 0)),
            pl.BlockSpec((1, K, bn), lambda i, j: (layer, 0, j)),
            pl.BlockSpec((1, K, bn), lambda i, j: (layer, 0, j + nj)),
            pl.BlockSpec((1, bn), lambda i, j: (0, j)),
            pl.BlockSpec((1, bn), lambda i, j: (0, j + nj)),
        ],
        out_specs=pl.BlockSpec((bm, bn), lambda i, j: (i, j)),
        out_shape=jax.ShapeDtypeStruct((M, N), F32),
        compiler_params=_params("parallel", "parallel"),
        name="mm_glu",
    )(h, w, w, b2, b2)


def _mm_resid_kernel(h_ref, w_ref, b_ref, x_ref, g_ref, o_ref):
    y = _dot(h_ref[...], w_ref[0]) + b_ref[...]
    o_ref[...] = x_ref[...] + g_ref[0] * y


def _mm_resid(h, w, layer, bias, xres, gate, rows_per_batch):
    M, K = h.shape
    N = w.shape[2]
    B = gate.shape[0]
    bm, bn = _tile(rows_per_batch, 1024), _tile(N, 1024)
    nb = rows_per_batch // bm
    return pl.pallas_call(
        _mm_resid_kernel,
        grid=(M // bm, N // bn),
        in_specs=[
            pl.BlockSpec((bm, K), lambda i, j: (i, 0)),
            pl.BlockSpec((1, K, bn), lambda i, j: (layer, 0, j)),
            pl.BlockSpec((1, bn), lambda i, j: (0, j)),
            pl.BlockSpec((bm, bn), lambda i, j: (i, j)),
            pl.BlockSpec((1, 1, bn), lambda i, j: (i // nb, 0, j)),
        ],
        out_specs=pl.BlockSpec((bm, bn), lambda i, j: (i, j)),
        out_shape=jax.ShapeDtypeStruct((M, N), F32),
        compiler_params=_params("parallel", "parallel"),
        name="mm_resid",
    )(h, w, bias.reshape(1, N), xres, gate.reshape(B, 1, N))


def _mm_relu2_kernel(h_ref, w_ref, o_ref):
    y = jnp.maximum(_dot(h_ref[...], w_ref[0].astype(BF16)), 0.0)
    o_ref[...] = (y * y).astype(o_ref.dtype)


def _mm_relu2(h, w, layer):
    M, K = h.shape
    N = w.shape[2]
    bm, bn = _tile(M, 1024), _tile(N, 512)
    return pl.pallas_call(
        _mm_relu2_kernel,
        grid=(M // bm, N // bn),
        in_specs=[
            pl.BlockSpec((bm, K), lambda i, j: (i, 0)),
            pl.BlockSpec((1, K, bn), lambda i, j: (layer, 0, j)),
        ],
        out_specs=pl.BlockSpec((bm, bn), lambda i, j: (i, j)),
        out_shape=jax.ShapeDtypeStruct((M, N), BF16),
        compiler_params=_params("parallel", "parallel"),
        name="mm_relu2",
    )(h, w)


def _mm_kacc_resid_kernel(h_ref, w_ref, x_ref, g_ref, o_ref, acc_ref):
    k = pl.program_id(2)

    @pl.when(k == 0)
    def _():
        acc_ref[...] = jnp.zeros_like(acc_ref)

    acc_ref[...] += _dot(h_ref[...], w_ref[0])

    @pl.when(k == pl.num_programs(2) - 1)
    def _():
        o_ref[...] = x_ref[...] + g_ref[0] * acc_ref[...]


def _mm_kacc_resid(h, w, layer, xres, gate, rows_per_batch):
    M, K = h.shape
    N = w.shape[2]
    B = gate.shape[0]
    bm, bn, bk = _tile(rows_per_batch, 1024), _tile(N, 1024), _tile(K, 2048)
    nb = rows_per_batch // bm
    return pl.pallas_call(
        _mm_kacc_resid_kernel,
        grid=(M // bm, N // bn, K // bk),
        in_specs=[
            pl.BlockSpec((bm, bk), lambda i, j, k: (i, k)),
            pl.BlockSpec((1, bk, bn), lambda i, j, k: (layer, k, j)),
            pl.BlockSpec((bm, bn), lambda i, j, k: (i, j)),
            pl.BlockSpec((1, 1, bn), lambda i, j, k: (i // nb, 0, j)),
        ],
        out_specs=pl.BlockSpec((bm, bn), lambda i, j, k: (i, j)),
        out_shape=jax.ShapeDtypeStruct((M, N), F32),
        scratch_shapes=[pltpu.VMEM((bm, bn), F32)],
        compiler_params=_params("parallel", "parallel", "arbitrary"),
        name="mm_kacc_resid",
    )(h, w, xres, gate.reshape(B, 1, N))


def _mm_q_kernel(h_ref, w_ref, g_ref, o_ref, *, dh, scale):
    acc = _dot(h_ref[...], w_ref[0])
    g = g_ref[...] * scale
    for c in range(acc.shape[1] // dh):
        blk = acc[:, c * dh:(c + 1) * dh]
        r = lax.rsqrt(jnp.mean(blk * blk, axis=-1, keepdims=True) + EPS)
        o_ref[:, c * dh:(c + 1) * dh] = (blk * r * g).astype(o_ref.dtype)


def _mm_q(h, w, layer, N, q_gain, dh):
    M, K = h.shape
    bm, bn = _tile(M, 1024), _tile(N, 1024)
    return pl.pallas_call(
        functools.partial(_mm_q_kernel, dh=dh, scale=float(dh) ** -0.5 * LOG2E),
        grid=(M // bm, N // bn),
        in_specs=[
            pl.BlockSpec((bm, K), lambda i, j: (i, 0)),
            pl.BlockSpec((1, K, bn), lambda i, j: (layer, 0, j)),
            pl.BlockSpec((1, dh), lambda i, j: (0, 0)),
        ],
        out_specs=pl.BlockSpec((bm, bn), lambda i, j: (i, j)),
        out_shape=jax.ShapeDtypeStruct((M, N), BF16),
        compiler_params=_params("parallel", "parallel"),
        name="mm_q",
    )(h, w, q_gain.reshape(1, dh))


def _mm_gate_kernel(h_ref, w_ref, o_ref):
    o_ref[...] = _sigmoid(_dot(h_ref[...], w_ref[...]))


def _mm_gate(h, w):
    M, K = h.shape
    N = w.shape[1]
    bm = _tile(M, 1024)
    return pl.pallas_call(
        _mm_gate_kernel,
        grid=(M // bm,),
        in_specs=[
            pl.BlockSpec((bm, K), lambda i: (i, 0)),
            pl.BlockSpec((K, N), lambda i: (0, 0)),
        ],
        out_specs=pl.BlockSpec((bm, N), lambda i: (i, 0)),
        out_shape=jax.ShapeDtypeStruct((M, N), F32),
        compiler_params=_params("parallel"),
        name="mm_gate",
    )(h, w)


def _mm_kv_kernel(h_ref, w_ref, gain_ref, flag_ref, o_ref, *, dh):
    acc = _dot(h_ref[...], w_ref[...])
    gain = gain_ref[0]
    use_norm = flag_ref[0] > 0.5
    for c in range(o_ref.shape[0]):
        blk = acc[:, c * dh:(c + 1) * dh]
        r = lax.rsqrt(jnp.mean(blk * blk, axis=-1, keepdims=True) + EPS)
        o_ref[c] = jnp.where(use_norm, blk * r * gain, blk).astype(o_ref.dtype)


def _mm_kv(h, w, gain, flag, G, dh):
    M, K = h.shape
    nbr = w.shape[1] // (G * dh)
    bm, bn = _tile(M, 1024), G * dh
    return pl.pallas_call(
        functools.partial(_mm_kv_kernel, dh=dh),
        grid=(M // bm, nbr),
        in_specs=[
            pl.BlockSpec((bm, K), lambda i, j: (i, 0)),
            pl.BlockSpec((K, bn), lambda i, j: (0, j)),
            pl.BlockSpec((1, 1, dh), lambda i, j: (j, 0, 0)),
            pl.BlockSpec((1, 1, dh), lambda i, j: (j, 0, 0)),
        ],
        out_specs=pl.BlockSpec((G, bm, dh), lambda i, j: (j, i, 0)),
        out_shape=jax.ShapeDtypeStruct((nbr * G, M, dh), BF16),
        compiler_params=_params("parallel", "parallel"),
        name="mm_kv",
    )(h, w, gain, flag)


def _dwconv_kernel(ucur_ref, uprev_ref, w_ref, bdw_ref, lng_ref, lnb_ref, o_ref, ext_ref, y_ref, *, tt, kc, D):
    i = pl.program_id(1)
    n = tt + CONV_HALO
    ext_ref[0:CONV_HALO, :] = jnp.where(i > 0, uprev_ref[0], 0.0)
    ext_ref[CONV_HALO:n, :] = ucur_ref[0]
    first = CONV_HALO - (kc - 1)
    nchunk = D // LANES

    def conv_chunk(c, carry):
        s1, s2 = carry
        col = pl.ds(pl.multiple_of(c * LANES, LANES), LANES)
        big = ext_ref[:, col]
        shifted = [big] + [pltpu.roll(big, n - s, axis=0) for s in range(1, SUBLANES)]
        acc = jnp.zeros((tt, LANES), F32)
        for k in range(kc):
            a, s = divmod(first + k, SUBLANES)
            acc = acc + w_ref[k:k + 1, col] * shifted[s][a * SUBLANES:a * SUBLANES + tt]
        y = acc + bdw_ref[:, col]
        y_ref[:, col] = y
        return s1 + y, s2 + y * y

    zero = jnp.zeros((tt, LANES), F32)
    s1, s2 = lax.fori_loop(0, nchunk, conv_chunk, (zero, zero))
    mean = jnp.sum(s1, axis=-1, keepdims=True) * (1.0 / D)
    var = jnp.maximum(jnp.sum(s2, axis=-1, keepdims=True) * (1.0 / D) - mean * mean, 0.0)
    rstd = lax.rsqrt(var + EPS)

    def norm_chunk(c, carry):
        col = pl.ds(pl.multiple_of(c * LANES, LANES), LANES)
        yn = (y_ref[:, col] - mean) * rstd * lng_ref[:, col] + lnb_ref[:, col]
        o_ref[0, :, col] = (yn * _sigmoid(yn)).astype(o_ref.dtype)
        return carry

    lax.fori_loop(0, nchunk, norm_chunk, 0)


def _dwconv_ln_silu(u, w_dw, b_dw, ln_g, ln_b):
    B, S, D = u.shape
    kc = w_dw.shape[0]
    assert kc - 1 <= CONV_HALO and D % LANES == 0
    tt = _tile(S, 128)
    assert tt % CONV_HALO == 0
    hb = tt // CONV_HALO
    return pl.pallas_call(
        functools.partial(_dwconv_kernel, tt=tt, kc=kc, D=D),
        grid=(B, S // tt),
        in_specs=[
            pl.BlockSpec((1, tt, D), lambda b, i: (b, i, 0)),
            pl.BlockSpec((1, CONV_HALO, D), lambda b, i: (b, jnp.maximum(i * hb - 1, 0), 0)),
            pl.BlockSpec((kc, D), lambda b, i: (0, 0)),
            pl.BlockSpec((1, D), lambda b, i: (0, 0)),
            pl.BlockSpec((1, D), lambda b, i: (0, 0)),
            pl.BlockSpec((1, D), lambda b, i: (0, 0)),
        ],
        out_specs=pl.BlockSpec((1, tt, D), lambda b, i: (b, i, 0)),
        out_shape=jax.ShapeDtypeStruct((B, S, D), BF16),
        scratch_shapes=[pltpu.VMEM((tt + CONV_HALO, D), F32), pltpu.VMEM((tt, D), F32)],
        compiler_params=_params("parallel", "parallel"),
        name="dwconv_ln_silu",
    )(u, u, w_dw, b_dw.reshape(1, D), ln_g.reshape(1, D), ln_b.reshape(1, D))


def _compress_kernel(t_ref, pos_ref, w1a_ref, w1b_ref, b1_ref, w2_ref, b2_ref, g_ref, o_ref):
    br = pl.program_id(0)
    t = t_ref[0].astype(F32)
    n = t.shape[0]
    pos = pos_ref[0]
    a = _dot((t + pos[0:1]).astype(BF16), w1a_ref[0])
    b = _dot((t + pos[1:2]).astype(BF16), w1b_ref[0])
    hid = a + pltpu.roll(b, n - 1, axis=0) + b1_ref[0]
    hid = hid * _sigmoid(hid)
    out = _dot(hid.astype(BF16), w2_ref[0]) + b2_ref[0]
    normed = out * lax.rsqrt(jnp.mean(out * out, axis=-1, keepdims=True) + EPS) * g_ref[...]
    out = jnp.where(br == 0, normed, out)
    row = lax.broadcasted_iota(jnp.int32, out.shape, 0)
    o_ref[0, 0, 0] = jnp.where(row < n - 1, out, 0.0).astype(o_ref.dtype)


def _compress(kvh, cmp_pos, cmp_w1, cmp_b1, cmp_w2, cmp_b2, k_gain0, B, G, S, dh):
    blk = cmp_pos.shape[1]
    assert blk == 2 * CMP_STRIDE
    half = CMP_STRIDE * dh
    n = S // CMP_STRIDE
    hid = cmp_w1.shape[2]
    t = kvh[:2 * G].reshape(2 * G, B * n, half)
    pos = cmp_pos.reshape(2, 2, half)
    w1 = cmp_w1.astype(BF16)
    return pl.pallas_call(
        _compress_kernel,
        grid=(2, B, G),
        in_specs=[
            pl.BlockSpec((1, n, half), lambda i, b, g: (i * G + g, b, 0)),
            pl.BlockSpec((1, 2, half), lambda i, b, g: (i, 0, 0)),
            pl.BlockSpec((1, half, hid), lambda i, b, g: (i, 0, 0)),
            pl.BlockSpec((1, half, hid), lambda i, b, g: (i, 1, 0)),
            pl.BlockSpec((1, 1, hid), lambda i, b, g: (i, 0, 0)),
            pl.BlockSpec((1, hid, dh), lambda i, b, g: (i, 0, 0)),
            pl.BlockSpec((1, 1, dh), lambda i, b, g: (i, 0, 0)),
            pl.BlockSpec((1, dh), lambda i, b, g: (0, 0)),
        ],
        out_specs=pl.BlockSpec((1, 1, 1, n, dh), lambda i, b, g: (i, b, g, 0, 0)),
        out_shape=jax.ShapeDtypeStruct((2, B, G, n, dh), BF16),
        compiler_params=_params("parallel", "parallel", "parallel"),
        name="compress_kv",
    )(t, pos, w1, w1, cmp_b1.reshape(2, 1, hid), cmp_w2.astype(BF16), cmp_b2.reshape(2, 1, dh),
      k_gain0.reshape(1, dh))


def _weights(s, bias, bounded):
    s = s + bias
    m = None if bounded else jnp.max(s, axis=0, keepdims=True)
    p = jnp.exp2(s if bounded else s - m)
    return p, jnp.sum(p, axis=0, keepdims=True), m


def _nsa_kernel(bound_ref, q_ref, gate_ref, kc_ref, vct_ref, ks_ref, vst_ref, kw_ref, vwt_ref, wselt_ref, et_ref,
                o_ref, *, T, R, dh, S, cmp_blk, bounded):
    q0 = pl.program_id(2) * T
    q = q_ref[0].astype(F32)
    qt = jnp.concatenate([q[:, r * dh:(r + 1) * dh].T for r in range(R)], axis=1).astype(BF16)
    t_row = q0 + lax.broadcasted_iota(jnp.int32, (1, T), 1)
    heads = [slice(r * T, (r + 1) * T) for r in range(R)]
    keep = -bound_ref[0, 0] if bounded else 0.0

    kc = kc_ref[0, 0, 0]
    ncp = kc.shape[0]
    s_c = _dot(kc, qt)
    cmp_end = lax.broadcasted_iota(jnp.int32, (ncp, 1), 0) * CMP_STRIDE + (cmp_blk - 1)
    bias_c = jnp.where(cmp_end <= t_row, keep, NEG)
    any_c = t_row >= cmp_blk - 1
    p_c = []
    for h in heads:
        p_r, l_r, _ = _weights(s_c[:, h], bias_c, bounded)
        p_c.append(p_r * jnp.where(any_c, 1.0 / jnp.maximum(l_r, TINY), 0.0))
    o_c = _dot(vct_ref[0], jnp.concatenate(p_c, axis=1).astype(BF16))

    nw = min(WINDOW + T, S)
    w0 = pl.multiple_of(jnp.clip(q0 - WINDOW, 0, S - nw), T)
    s_w = _dot(kw_ref[0, pl.ds(w0, nw), :], qt)
    diff = t_row - (w0 + lax.broadcasted_iota(jnp.int32, (nw, 1), 0))
    bias_w = jnp.where((diff >= 0) & (diff < WINDOW), keep, NEG)
    p_w, inv_w = [], []
    for h in heads:
        p_r, l_r, _ = _weights(s_w[:, h], bias_w, bounded)
        inv_w.append(1.0 / jnp.maximum(l_r, TINY))
        p_w.append(p_r.astype(BF16))
    o_w = _dot(vwt_ref[0, :, pl.ds(w0, nw)], jnp.concatenate(p_w, axis=1)) * jnp.concatenate(inv_w, axis=1)

    p_sum = p_c[0]
    for r in range(1, R):
        p_sum = p_sum + p_c[r]
    imp = _dot(wselt_ref[...], p_sum.astype(BF16))
    ns = imp.shape[0]
    jblk = lax.broadcasted_iota(jnp.int32, (ns, 1), 0)
    cur = t_row // SEL_BLK
    forced = (jblk == 0) | (jblk == cur) | (jblk == cur - 1)
    st = jnp.where(forced, BIG, imp)
    st = jnp.where(jblk <= cur, st, -BIG)
    jrow = lax.broadcasted_iota(jnp.int32, (ns, T), 0).astype(F32)
    sel = jnp.zeros((ns, T), F32)
    for _ in range(min(N_SEL, ns)):
        top = jnp.max(st, axis=0, keepdims=True)
        first = jnp.min(jnp.where(st == top, jrow, float(ns)), axis=0, keepdims=True)
        hit = jrow == first
        sel = jnp.where(hit, 1.0, sel)
        st = jnp.where(hit, NEG, st)
    chosen = _dot(et_ref[pl.ds(q0, T), :], sel.astype(BF16))
    kpos = q0 + lax.broadcasted_iota(jnp.int32, (T, 1), 0)
    bias_d = jnp.where((chosen > 0.5) & (kpos <= t_row), keep, NEG)
    s_d = _dot(ks_ref[0, pl.ds(q0, T), :], qt)
    first_w = [_weights(s_d[:, h], bias_d, bounded) for h in heads]
    l0 = jnp.concatenate([w[1] for w in first_w], axis=1)
    acc0 = _dot(vst_ref[0, :, pl.ds(q0, T)], jnp.concatenate([w[0].astype(BF16) for w in first_w], axis=1))

    before = jblk * SEL_BLK < q0
    sel_bias = jnp.where((sel > 0.5) & before, keep, NEG).astype(BF16)
    qx = jnp.concatenate([qt, jnp.concatenate([sel_bias] * R, axis=1)], axis=0)
    tk = min(ATT_KV_TILE, S)
    n_tiles = (q0 + tk - 1) // tk

    def masked_scores(kt):
        k0 = pl.multiple_of(kt * tk, tk)
        kx = jnp.concatenate([ks_ref[0, pl.ds(k0, tk), :], et_ref[pl.ds(k0, tk), :]], axis=1)
        return k0, _dot(kx, qx)

    if bounded:
        def kv_step(kt, carry):
            l, acc = carry
            k0, s = masked_scores(kt)
            p = [jnp.exp2(s[:, h]) for h in heads]
            l = l + jnp.concatenate([jnp.sum(p_r, axis=0, keepdims=True) for p_r in p], axis=1)
            pt = jnp.concatenate([p_r.astype(BF16) for p_r in p], axis=1)
            return l, acc + _dot(vst_ref[0, :, pl.ds(k0, tk)], pt)

        l_s, acc_s = lax.fori_loop(0, n_tiles, kv_step, (l0, acc0))
    else:
        def kv_step(kt, carry):
            m, l, acc = carry
            k0, s = masked_scores(kt)
            m_new, p = [], []
            for h in heads:
                m_r = jnp.maximum(m[:, h], jnp.max(s[:, h], axis=0, keepdims=True))
                p.append(jnp.exp2(s[:, h] - m_r))
                m_new.append(m_r)
            m_new = jnp.concatenate(m_new, axis=1)
            alpha = jnp.exp2(m - m_new)
            l_new = alpha * l + jnp.concatenate([jnp.sum(p_r, axis=0, keepdims=True) for p_r in p], axis=1)
            pt = jnp.concatenate([p_r.astype(BF16) for p_r in p], axis=1)
            return m_new, l_new, alpha * acc + _dot(vst_ref[0, :, pl.ds(k0, tk)], pt)

        m0 = jnp.concatenate([w[2] for w in first_w], axis=1)
        _, l_s, acc_s = lax.fori_loop(0, n_tiles, kv_step, (m0, l0, acc0))
    o_s = acc_s * (1.0 / jnp.maximum(l_s, TINY))

    gt = gate_ref[0].T
    out = []
    for r, h in enumerate(heads):
        c0 = N_BRANCH * r
        o_r = gt[c0:c0 + 1] * o_c[:, h] + gt[c0 + 1:c0 + 2] * o_s[:, h] + gt[c0 + 2:c0 + 3] * o_w[:, h]
        out.append(o_r.T)
    o_ref[0] = jnp.concatenate(out, axis=1).astype(o_ref.dtype)


def _nsa_attention(q, gates, cmp_kv, kvh, wselt, et, score_bound, B, G, R, S, dh, cmp_blk):
    T = _tile(S, ATT_Q_TILE)
    ns, ncp = wselt.shape
    vct = jnp.swapaxes(cmp_kv[1], -1, -2).reshape(B * G, dh, ncp)
    vst = jnp.swapaxes(kvh[3 * G:4 * G], -1, -2)
    vwt = jnp.swapaxes(kvh[5 * G:6 * G], -1, -2)
    shift = jnp.ceil(score_bound * 1.05) + 2.0

    def k_spec(br):
        return pl.BlockSpec((1, S, dh), lambda b, g, i: (br * G + g, b, 0))

    vt_spec = pl.BlockSpec((1, dh, S), lambda b, g, i: (g, 0, b))

    def run(bounded):
        return pl.pallas_call(
            functools.partial(_nsa_kernel, T=T, R=R, dh=dh, S=S, cmp_blk=cmp_blk, bounded=bounded),
            grid=(B, G, S // T),
            in_specs=[
                pl.BlockSpec(memory_space=pltpu.SMEM),
                pl.BlockSpec((1, T, R * dh), lambda b, g, i: (b, i, g)),
                pl.BlockSpec((1, T, LANES), lambda b, g, i: (b, i, g)),
                pl.BlockSpec((1, 1, 1, ncp, dh), lambda b, g, i: (0, b, g, 0, 0)),
                pl.BlockSpec((1, dh, ncp), lambda b, g, i: (b * G + g, 0, 0)),
                k_spec(2), vt_spec, k_spec(4), vt_spec,
                pl.BlockSpec((ns, ncp), lambda b, g, i: (0, 0)),
                pl.BlockSpec((S, ns), lambda b, g, i: (0, 0)),
            ],
            out_specs=pl.BlockSpec((1, T, R * dh), lambda b, g, i: (b, i, g)),
            out_shape=jax.ShapeDtypeStruct((B, S, G * R * dh), BF16),
            compiler_params=_params("parallel", "parallel", "arbitrary"),
            name="nsa_attention_bounded" if bounded else "nsa_attention",
        )(shift.reshape(1, 1), q, gates, cmp_kv, vct, kvh, vst, kvh, vwt, wselt, et)

    return lax.cond(shift <= MAX_SCORE_SHIFT, lambda: run(True), lambda: run(False))


def _selection_constants(S, ncp, cmp_blk):
    ns = S // SEL_BLK
    nc = (S - cmp_blk) // CMP_STRIDE + 1
    cs = jnp.arange(ncp) * CMP_STRIDE
    ss = jnp.arange(ns) * SEL_BLK
    ov = jnp.clip(jnp.minimum(cs[None, :] + cmp_blk, ss[:, None] + SEL_BLK) - jnp.maximum(cs[None, :], ss[:, None]),
                  0, None)
    wselt = jnp.where(jnp.arange(ncp)[None, :] < nc, ov / CMP_STRIDE, 0.0).astype(BF16)
    et = (jnp.arange(S)[:, None] // SEL_BLK == jnp.arange(ns)[None, :]).astype(BF16)
    return wselt, et


def kernel(x, c, w_ada, b_ada, ada_table, norm_mix_g, norm_ffn_g, w_ffn1, w_ffn2,
           conv_w_pw1, conv_b_pw1, conv_w_dw, conv_b_dw, conv_ln_g, conv_ln_b, conv_w_pw2, conv_b_pw2,
           kv_norm_g, w_kv, k_norm_g, cmp_pos, cmp_w1, cmp_b1, cmp_w2, cmp_b2,
           attn_w_qg, attn_q_norm_g, attn_w_o):
    B, S, D = x.shape
    M = B * S
    depth = ada_table.shape[0]
    n_a = conv_w_pw1.shape[0]
    dh = k_norm_g.shape[1]
    G = w_kv.shape[1] // (2 * N_BRANCH * dh)
    HD = attn_w_o.shape[1]
    R = HD // (G * dh)
    cmp_blk = cmp_pos.shape[1]
    assert R * N_BRANCH <= LANES and S % SEL_BLK == 0 and S % CMP_STRIDE == 0

    mod = _ada_mod(c, w_ada, b_ada, ada_table).reshape(depth, B, N_MOD, D)
    x2 = x.reshape(M, D)
    zeros_bd = jnp.zeros((B, D), F32)
    w_pw1, w_pw2 = conv_w_pw1.astype(BF16), conv_w_pw2.astype(BF16)
    w_qg, w_o = attn_w_qg.astype(BF16), attn_w_o.astype(BF16)
    w_ffn2_bf = w_ffn2.astype(BF16)
    kvh = cmp_kv = None
    wselt, et = _selection_constants(S, S // CMP_STRIDE, cmp_blk)

    for l in range(depth):
        sh_m, sc_m, g_m, sh_f, sc_f, g_f = (mod[l, :, i] for i in range(N_MOD))
        h = _norm_mod(x2.reshape(B, S, D), norm_mix_g[l], sc_m, sh_m)
        if l < n_a:
            u = _mm_glu(h.reshape(M, D), w_pw1, l, conv_b_pw1[l])
            v = _dwconv_ln_silu(u.reshape(B, S, D), conv_w_dw[l], conv_b_dw[l], conv_ln_g[l], conv_ln_b[l])
            x2 = _mm_resid(v.reshape(M, D), w_pw2, l, conv_b_pw2[l], x2, g_m, S)
        else:
            i = l - n_a
            hm = h.reshape(M, D)
            q = _mm_q(hm, w_qg, i, HD, attn_q_norm_g[i], dh)
            wg = attn_w_qg[i][:, HD:].reshape(D, G, R * N_BRANCH)
            wg = jnp.pad(wg, ((0, 0), (0, 0), (0, LANES - R * N_BRANCH))).reshape(D, G * LANES)
            gates = _mm_gate(hm, wg.astype(BF16))
            score_bound = (float(dh) ** 0.5 * LOG2E) * jnp.max(jnp.abs(attn_q_norm_g[i])) * jnp.max(jnp.abs(k_norm_g))
            o = _nsa_attention(q.reshape(B, S, HD), gates.reshape(B, S, G * LANES), cmp_kv, kvh, wselt, et,
                               score_bound, B, G, R, S, dh, cmp_blk)
            x2 = _mm_resid(o.reshape(M, HD), w_o, i, jnp.zeros((D,), F32), x2, g_m, S)
        h = _norm_mod(x2.reshape(B, S, D), norm_ffn_g[l], sc_f, sh_f)
        hid = _mm_relu2(h.reshape(M, D), w_ffn1, l)
        x2 = _mm_kacc_resid(hid, w_ffn2_bf, l, x2, g_f, S)
        if l == n_a - 1:
            hk = _norm_mod(x2.reshape(B, S, D), kv_norm_g, zeros_bd, zeros_bd)
            ones = jnp.ones((dh,), F32)
            gain = jnp.stack([ones, ones, k_norm_g[1], ones, k_norm_g[2], ones]).reshape(2 * N_BRANCH, 1, dh)
            flag = jnp.array([0.0, 0.0, 1.0, 0.0, 1.0, 0.0], F32)[:, None, None] * jnp.ones((1, 1, dh), F32)
            kvh = _mm_kv(hk.reshape(M, D), w_kv.astype(BF16), gain, flag, G, dh)
            cmp_kv = _compress(kvh, cmp_pos, cmp_w1, cmp_b1, cmp_w2, cmp_b2, k_norm_g[0], B, G, S, dh)
    return x2.reshape(B, S, D)
```

```python
import functools

import jax
import jax.numpy as jnp
from jax import lax
from jax.experimental import pallas as pl
from jax.experimental.pallas import tpu as pltpu

F32 = jnp.float32
BF16 = jnp.bfloat16

CMP_STRIDE = 16
SEL_BLK = 64
N_SEL = 16
WINDOW = 512
N_BRANCH = 3
N_MOD = 6
EPS = 1e-6
NEG = -1e30
BIG = 1e9
TINY = 1e-20
LOG2E = 1.4426950408889634
MAX_SCORE_SHIFT = 50.0

LANES = 128
SUBLANES = 8
VMEM_LIMIT_BYTES = 56 * 1024 * 1024

ATT_Q_TILE = 256
ATT_KV_TILE = 1024
CONV_HALO = 32


def _tile(dim, pref):
    t = min(dim, pref)
    assert dim % t == 0, (dim, pref)
    return t


def _params(*sem):
    return pltpu.CompilerParams(dimension_semantics=sem, vmem_limit_bytes=VMEM_LIMIT_BYTES)


def _dot(a, b):
    return jnp.dot(a, b, preferred_element_type=F32)


def _sigmoid(x):
    return 1.0 / (1.0 + jnp.exp(-x))


def _ada_kernel(c_ref, w_ref, b_ref, tab_ref, o_ref):
    c = c_ref[...]
    a = (c * _sigmoid(c)).astype(BF16)
    base = _dot(a, w_ref[...].astype(BF16)) + b_ref[...]
    for l in range(o_ref.shape[0]):
        o_ref[l] = base + tab_ref[l]


def _ada_mod(c, w_ada, b_ada, ada_table):
    B, D = c.shape
    depth = ada_table.shape[0]
    N = w_ada.shape[1]
    bn = _tile(N, 1024)
    return pl.pallas_call(
        _ada_kernel,
        grid=(N // bn,),
        in_specs=[
            pl.BlockSpec((B, D), lambda j: (0, 0)),
            pl.BlockSpec((D, bn), lambda j: (0, j)),
            pl.BlockSpec((1, bn), lambda j: (0, j)),
            pl.BlockSpec((depth, 1, bn), lambda j: (0, 0, j)),
        ],
        out_specs=pl.BlockSpec((depth, B, bn), lambda j: (0, 0, j)),
        out_shape=jax.ShapeDtypeStruct((depth, B, N), F32),
        compiler_params=_params("parallel"),
        name="ada_mod",
    )(c, w_ada, b_ada.reshape(1, N), ada_table.reshape(depth, 1, N))


def _norm_mod_kernel(x_ref, g_ref, sc_ref, sh_ref, o_ref):
    x = x_ref[0]
    y = x * lax.rsqrt(jnp.mean(x * x, axis=-1, keepdims=True) + EPS)
    o_ref[0] = ((y * g_ref[...]) * (1.0 + sc_ref[0]) + sh_ref[0]).astype(o_ref.dtype)


def _norm_mod(x, g, scale, shift):
    B, S, D = x.shape
    tm = _tile(S, 256)
    return pl.pallas_call(
        _norm_mod_kernel,
        grid=(B, S // tm),
        in_specs=[
            pl.BlockSpec((1, tm, D), lambda b, i: (b, i, 0)),
            pl.BlockSpec((1, D), lambda b, i: (0, 0)),
            pl.BlockSpec((1, 1, D), lambda b, i: (b, 0, 0)),
            pl.BlockSpec((1, 1, D), lambda b, i: (b, 0, 0)),
        ],
        out_specs=pl.BlockSpec((1, tm, D), lambda b, i: (b, i, 0)),
        out_shape=jax.ShapeDtypeStruct((B, S, D), BF16),
        compiler_params=_params("parallel", "parallel"),
        name="norm_mod",
    )(x, g.reshape(1, D), scale.reshape(B, 1, D), shift.reshape(B, 1, D))


def _mm_glu_kernel(h_ref, wa_ref, wg_ref, ba_ref, bg_ref, o_ref):
    h = h_ref[...]
    a = _dot(h, wa_ref[0]) + ba_ref[...]
    gt = _dot(h, wg_ref[0]) + bg_ref[...]
    o_ref[...] = a * _sigmoid(gt)


def _mm_glu(h, w, layer, b):
    M, K = h.shape
    N = w.shape[2] // 2
    bm, bn = _tile(M, 1024), _tile(N, 512)
    nj = N // bn
    b2 = b.reshape(1, 2 * N)
    return pl.pallas_call(
        _mm_glu_kernel,
        grid=(M // bm, nj),
        in_specs=[
            pl.BlockSpec((bm, K), lambda i, j: (i, 0)),
            pl.BlockSpec((1, K, bn), lambda i, j: (layer, 0, j)),
            pl.BlockSpec((1, K, bn), lambda i, j: (layer, 0, j + nj)),
            pl.BlockSpec((1, bn), lambda i, j: (0, j)),
            pl.BlockSpec((1, bn), lambda i, j: (0, j + nj)),
        ],
        out_specs=pl.BlockSpec((bm, bn), lambda i, j: (i, j)),
        out_shape=jax.ShapeDtypeStruct((M, N), F32),
        compiler_params=_params("parallel", "parallel"),
        name="mm_glu",
    )(h, w, w, b2, b2)


def _mm_resid_kernel(h_ref, w_ref, b_ref, x_ref, g_ref, o_ref):
    y = _dot(h_ref[...], w_ref[0]) + b_ref[...]
    o_ref[...] = x_ref[...] + g_ref[0] * y


def _mm_resid(h, w, layer, bias, xres, gate, rows_per_batch):
    M, K = h.shape
    N = w.shape[2]
    B = gate.shape[0]
    bm, bn = _tile(rows_per_batch, 1024), _tile(N, 1024)
    nb = rows_per_batch // bm
    return pl.pallas_call(
        _mm_resid_kernel,
        grid=(M // bm, N // bn),
        in_specs=[
            pl.BlockSpec((bm, K), lambda i, j: (i, 0)),
            pl.BlockSpec((1, K, bn), lambda i, j: (layer, 0, j)),
            pl.BlockSpec((1, bn), lambda i, j: (0, j)),
            pl.BlockSpec((bm, bn), lambda i, j: (i, j)),
            pl.BlockSpec((1, 1, bn), lambda i, j: (i // nb, 0, j)),
        ],
        out_specs=pl.BlockSpec((bm, bn), lambda i, j: (i, j)),
        out_shape=jax.ShapeDtypeStruct((M, N), F32),
        compiler_params=_params("parallel", "parallel"),
        name="mm_resid",
    )(h, w, bias.reshape(1, N), xres, gate.reshape(B, 1, N))


def _mm_relu2_kernel(h_ref, w_ref, o_ref):
    y = jnp.maximum(_dot(h_ref[...], w_ref[0].astype(BF16)), 0.0)
    o_ref[...] = (y * y).astype(o_ref.dtype)


def _mm_relu2(h, w, layer):
    M, K = h.shape
    N = w.shape[2]
    bm, bn = _tile(M, 2048), _tile(N, 512)
    return pl.pallas_call(
        _mm_relu2_kernel,
        grid=(M // bm, N // bn),
        in_specs=[
            pl.BlockSpec((bm, K), lambda i, j: (i, 0)),
            pl.BlockSpec((1, K, bn), lambda i, j: (layer, 0, j)),
        ],
        out_specs=pl.BlockSpec((bm, bn), lambda i, j: (i, j)),
        out_shape=jax.ShapeDtypeStruct((M, N), BF16),
        compiler_params=_params("parallel", "parallel"),
        name="mm_relu2",
    )(h, w)


def _mm_kacc_resid_kernel(h_ref, w_ref, x_ref, g_ref, o_ref, acc_ref):
    k = pl.program_id(2)

    @pl.when(k == 0)
    def _():
        acc_ref[...] = jnp.zeros_like(acc_ref)

    acc_ref[...] += _dot(h_ref[...], w_ref[0])

    @pl.when(k == pl.num_programs(2) - 1)
    def _():
        o_ref[...] = x_ref[...] + g_ref[0] * acc_ref[...]


def _mm_kacc_resid(h, w, layer, xres, gate, rows_per_batch):
    M, K = h.shape
    N = w.shape[2]
    B = gate.shape[0]
    bm, bn, bk = _tile(rows_per_batch, 1024), _tile(N, 1024), _tile(K, 2048)
    nb = rows_per_batch // bm
    return pl.pallas_call(
        _mm_kacc_resid_kernel,
        grid=(M // bm, N // bn, K // bk),
        in_specs=[
            pl.BlockSpec((bm, bk), lambda i, j, k: (i, k)),
            pl.BlockSpec((1, bk, bn), lambda i, j, k: (layer, k, j)),
            pl.BlockSpec((bm, bn), lambda i, j, k: (i, j)),
            pl.BlockSpec((1, 1, bn), lambda i, j, k: (i // nb, 0, j)),
        ],
        out_specs=pl.BlockSpec((bm, bn), lambda i, j, k: (i, j)),
        out_shape=jax.ShapeDtypeStruct((M, N), F32),
        scratch_shapes=[pltpu.VMEM((bm, bn), F32)],
        compiler_params=_params("parallel", "parallel", "arbitrary"),
        name="mm_kacc_resid",
    )(h, w, xres, gate.reshape(B, 1, N))


def _mm_q_kernel(h_ref, w_ref, g_ref, o_ref, *, dh, scale):
    acc = _dot(h_ref[...], w_ref[0])
    g = g_ref[...] * scale
    for c in range(acc.shape[1] // dh):
        blk = acc[:, c * dh:(c + 1) * dh]
        r = lax.rsqrt(jnp.mean(blk * blk, axis=-1, keepdims=True) + EPS)
        o_ref[:, c * dh:(c + 1) * dh] = (blk * r * g).astype(o_ref.dtype)


def _mm_q(h, w, layer, N, q_gain, dh):
    M, K = h.shape
    bm, bn = _tile(M, 1024), _tile(N, 1024)
    return pl.pallas_call(
        functools.partial(_mm_q_kernel, dh=dh, scale=float(dh) ** -0.5 * LOG2E),
        grid=(M // bm, N // bn),
        in_specs=[
            pl.BlockSpec((bm, K), lambda i, j: (i, 0)),
            pl.BlockSpec((1, K, bn), lambda i, j: (layer, 0, j)),
            pl.BlockSpec((1, dh), lambda i, j: (0, 0)),
        ],
        out_specs=pl.BlockSpec((bm, bn), lambda i, j: (i, j)),
        out_shape=jax.ShapeDtypeStruct((M, N), BF16),
        compiler_params=_params("parallel", "parallel"),
        name="mm_q",
    )(h, w, q_gain.reshape(1, dh))


def _mm_gate_kernel(h_ref, w_ref, o_ref):
    o_ref[...] = _sigmoid(_dot(h_ref[...], w_ref[...]))


def _mm_gate(h, w):
    M, K = h.shape
    N = w.shape[1]
    bm = _tile(M, 1024)
    return pl.pallas_call(
        _mm_gate_kernel,
        grid=(M // bm,),
        in_specs=[
            pl.BlockSpec((bm, K), lambda i: (i, 0)),
            pl.BlockSpec((K, N), lambda i: (0, 0)),
        ],
        out_specs=pl.BlockSpec((bm, N), lambda i: (i, 0)),
        out_shape=jax.ShapeDtypeStruct((M, N), F32),
        compiler_params=_params("parallel"),
        name="mm_gate",
    )(h, w)


def _mm_kv_kernel(h_ref, w_ref, gain_ref, flag_ref, o_ref, *, dh):
    acc = _dot(h_ref[...], w_ref[...])
    gain = gain_ref[0]
    use_norm = flag_ref[0] > 0.5
    for c in range(o_ref.shape[0]):
        blk = acc[:, c * dh:(c + 1) * dh]
        r = lax.rsqrt(jnp.mean(blk * blk, axis=-1, keepdims=True) + EPS)
        o_ref[c] = jnp.where(use_norm, blk * r * gain, blk).astype(o_ref.dtype)


def _mm_kv(h, w, gain, flag, G, dh):
    M, K = h.shape
    nbr = w.shape[1] // (G * dh)
    bm, bn = _tile(M, 1024), G * dh
    return pl.pallas_call(
        functools.partial(_mm_kv_kernel, dh=dh),
        grid=(M // bm, nbr),
        in_specs=[
            pl.BlockSpec((bm, K), lambda i, j: (i, 0)),
            pl.BlockSpec((K, bn), lambda i, j: (0, j)),
            pl.BlockSpec((1, 1, dh), lambda i, j: (j, 0, 0)),
            pl.BlockSpec((1, 1, dh), lambda i, j: (j, 0, 0)),
        ],
        out_specs=pl.BlockSpec((G, bm, dh), lambda i, j: (j, i, 0)),
        out_shape=jax.ShapeDtypeStruct((nbr * G, M, dh), BF16),
        compiler_params=_params("parallel", "parallel"),
        name="mm_kv",
    )(h, w, gain, flag)


def _dwconv_kernel(ucur_ref, uprev_ref, w_ref, bdw_ref, lng_ref, lnb_ref, o_ref, ext_ref, y_ref, *, tt, kc, D):
    i = pl.program_id(1)
    n = tt + CONV_HALO
    ext_ref[0:CONV_HALO, :] = jnp.where(i > 0, uprev_ref[0], 0.0)
    ext_ref[CONV_HALO:n, :] = ucur_ref[0]
    first = CONV_HALO - (kc - 1)
    nchunk = D // LANES

    def conv_chunk(c, carry):
        s1, s2 = carry
        col = pl.ds(pl.multiple_of(c * LANES, LANES), LANES)
        big = ext_ref[:, col]
        shifted = [big] + [pltpu.roll(big, n - s, axis=0) for s in range(1, SUBLANES)]
        acc = jnp.zeros((tt, LANES), F32)
        for k in range(kc):
            a, s = divmod(first + k, SUBLANES)
            acc = acc + w_ref[k:k + 1, col] * shifted[s][a * SUBLANES:a * SUBLANES + tt]
        y = acc + bdw_ref[:, col]
        y_ref[:, col] = y
        return s1 + y, s2 + y * y

    zero = jnp.zeros((tt, LANES), F32)
    s1, s2 = lax.fori_loop(0, nchunk, conv_chunk, (zero, zero))
    mean = jnp.sum(s1, axis=-1, keepdims=True) * (1.0 / D)
    var = jnp.maximum(jnp.sum(s2, axis=-1, keepdims=True) * (1.0 / D) - mean * mean, 0.0)
    rstd = lax.rsqrt(var + EPS)

    def norm_chunk(c, carry):
        col = pl.ds(pl.multiple_of(c * LANES, LANES), LANES)
        yn = (y_ref[:, col] - mean) * rstd * lng_ref[:, col] + lnb_ref[:, col]
        o_ref[0, :, col] = (yn * _sigmoid(yn)).astype(o_ref.dtype)
        return carry

    lax.fori_loop(0, nchunk, norm_chunk, 0)


def _dwconv_ln_silu(u, w_dw, b_dw, ln_g, ln_b):
    B, S, D = u.shape
    kc = w_dw.shape[0]
    assert kc - 1 <= CONV_HALO and D % LANES == 0
    tt = _tile(S, 256)
    assert tt % CONV_HALO == 0
    hb = tt // CONV_HALO
    return pl.pallas_call(
        functools.partial(_dwconv_kernel, tt=tt, kc=kc, D=D),
        grid=(B, S // tt),
        in_specs=[
            pl.BlockSpec((1, tt, D), lambda b, i: (b, i, 0)),
            pl.BlockSpec((1, CONV_HALO, D), lambda b, i: (b, jnp.maximum(i * hb - 1, 0), 0)),
            pl.BlockSpec((kc, D), lambda b, i: (0, 0)),
            pl.BlockSpec((1, D), lambda b, i: (0, 0)),
            pl.BlockSpec((1, D), lambda b, i: (0, 0)),
            pl.BlockSpec((1, D), lambda b, i: (0, 0)),
        ],
        out_specs=pl.BlockSpec((1, tt, D), lambda b, i: (b, i, 0)),
        out_shape=jax.ShapeDtypeStruct((B, S, D), BF16),
        scratch_shapes=[pltpu.VMEM((tt + CONV_HALO, D), F32), pltpu.VMEM((tt, D), F32)],
        compiler_params=_params("parallel", "parallel"),
        name="dwconv_ln_silu",
    )(u, u, w_dw, b_dw.reshape(1, D), ln_g.reshape(1, D), ln_b.reshape(1, D))


def _compress_kernel(t_ref, pos_ref, w1a_ref, w1b_ref, b1_ref, w2_ref, b2_ref, g_ref, o_ref):
    br = pl.program_id(0)
    t = t_ref[0].astype(F32)
    n = t.shape[0]
    pos = pos_ref[0]
    a = _dot((t + pos[0:1]).astype(BF16), w1a_ref[0])
    b = _dot((t + pos[1:2]).astype(BF16), w1b_ref[0])
    hid = a + pltpu.roll(b, n - 1, axis=0) + b1_ref[0]
    hid = hid * _sigmoid(hid)
    out = _dot(hid.astype(BF16), w2_ref[0]) + b2_ref[0]
    normed = out * lax.rsqrt(jnp.mean(out * out, axis=-1, keepdims=True) + EPS) * g_ref[...]
    out = jnp.where(br == 0, normed, out)
    row = lax.broadcasted_iota(jnp.int32, out.shape, 0)
    o_ref[0, 0, 0] = jnp.where(row < n - 1, out, 0.0).astype(o_ref.dtype)


def _compress(kvh, cmp_pos, cmp_w1, cmp_b1, cmp_w2, cmp_b2, k_gain0, B, G, S, dh):
    blk = cmp_pos.shape[1]
    assert blk == 2 * CMP_STRIDE
    half = CMP_STRIDE * dh
    n = S // CMP_STRIDE
    hid = cmp_w1.shape[2]
    t = kvh[:2 * G].reshape(2 * G, B * n, half)
    pos = cmp_pos.reshape(2, 2, half)
    w1 = cmp_w1.astype(BF16)
    return pl.pallas_call(
        _compress_kernel,
        grid=(2, B, G),
        in_specs=[
            pl.BlockSpec((1, n, half), lambda i, b, g: (i * G + g, b, 0)),
            pl.BlockSpec((1, 2, half), lambda i, b, g: (i, 0, 0)),
            pl.BlockSpec((1, half, hid), lambda i, b, g: (i, 0, 0)),
            pl.BlockSpec((1, half, hid), lambda i, b, g: (i, 1, 0)),
            pl.BlockSpec((1, 1, hid), lambda i, b, g: (i, 0, 0)),
            pl.BlockSpec((1, hid, dh), lambda i, b, g: (i, 0, 0)),
            pl.BlockSpec((1, 1, dh), lambda i, b, g: (i, 0, 0)),
            pl.BlockSpec((1, dh), lambda i, b, g: (0, 0)),
        ],
        out_specs=pl.BlockSpec((1, 1, 1, n, dh), lambda i, b, g: (i, b, g, 0, 0)),
        out_shape=jax.ShapeDtypeStruct((2, B, G, n, dh), BF16),
        compiler_params=_params("parallel", "parallel", "parallel"),
        name="compress_kv",
    )(t, pos, w1, w1, cmp_b1.reshape(2, 1, hid), cmp_w2.astype(BF16), cmp_b2.reshape(2, 1, dh),
      k_gain0.reshape(1, dh))


def _weights(s, bias, bounded):
    s = s + bias
    m = None if bounded else jnp.max(s, axis=0, keepdims=True)
    p = jnp.exp2(s if bounded else s - m)
    return p, jnp.sum(p, axis=0, keepdims=True), m


def _nsa_kernel(bound_ref, q_ref, gate_ref, kc_ref, vct_ref, ks_ref, vst_ref, kw_ref, vwt_ref, wselt_ref, et_ref,
                o_ref, *, T, R, dh, S, cmp_blk, bounded):
    q0 = pl.program_id(2) * T
    q = q_ref[0].astype(F32)
    qt = jnp.concatenate([q[:, r * dh:(r + 1) * dh].T for r in range(R)], axis=1).astype(BF16)
    t_row = q0 + lax.broadcasted_iota(jnp.int32, (1, T), 1)
    heads = [slice(r * T, (r + 1) * T) for r in range(R)]
    keep = -bound_ref[0, 0] if bounded else 0.0

    kc = kc_ref[0, 0, 0]
    ncp = kc.shape[0]
    s_c = _dot(kc, qt)
    cmp_end = lax.broadcasted_iota(jnp.int32, (ncp, 1), 0) * CMP_STRIDE + (cmp_blk - 1)
    bias_c = jnp.where(cmp_end <= t_row, keep, NEG)
    any_c = t_row >= cmp_blk - 1
    p_c = []
    for h in heads:
        p_r, l_r, _ = _weights(s_c[:, h], bias_c, bounded)
        p_c.append(p_r * jnp.where(any_c, 1.0 / jnp.maximum(l_r, TINY), 0.0))
    o_c = _dot(vct_ref[0], jnp.concatenate(p_c, axis=1).astype(BF16))

    nw = min(WINDOW + T, S)
    w0 = pl.multiple_of(jnp.clip(q0 - WINDOW, 0, S - nw), T)
    s_w = _dot(kw_ref[0, pl.ds(w0, nw), :], qt)
    diff = t_row - (w0 + lax.broadcasted_iota(jnp.int32, (nw, 1), 0))
    bias_w = jnp.where((diff >= 0) & (diff < WINDOW), keep, NEG)
    p_w, inv_w = [], []
    for h in heads:
        p_r, l_r, _ = _weights(s_w[:, h], bias_w, bounded)
        inv_w.append(1.0 / jnp.maximum(l_r, TINY))
        p_w.append(p_r.astype(BF16))
    o_w = _dot(vwt_ref[0, :, pl.ds(w0, nw)], jnp.concatenate(p_w, axis=1)) * jnp.concatenate(inv_w, axis=1)

    p_sum = p_c[0]
    for r in range(1, R):
        p_sum = p_sum + p_c[r]
    imp = _dot(wselt_ref[...], p_sum.astype(BF16))
    ns = imp.shape[0]
    jblk = lax.broadcasted_iota(jnp.int32, (ns, 1), 0)
    cur = t_row // SEL_BLK
    forced = (jblk == 0) | (jblk == cur) | (jblk == cur - 1)
    st = jnp.where(forced, BIG, imp)
    st = jnp.where(jblk <= cur, st, -BIG)
    jrow = lax.broadcasted_iota(jnp.int32, (ns, T), 0).astype(F32)
    sel = jnp.zeros((ns, T), F32)
    for _ in range(min(N_SEL, ns)):
        top = jnp.max(st, axis=0, keepdims=True)
        first = jnp.min(jnp.where(st == top, jrow, float(ns)), axis=0, keepdims=True)
        hit = jrow == first
        sel = jnp.where(hit, 1.0, sel)
        st = jnp.where(hit, NEG, st)
    chosen = _dot(et_ref[pl.ds(q0, T), :], sel.astype(BF16))
    kpos = q0 + lax.broadcasted_iota(jnp.int32, (T, 1), 0)
    bias_d = jnp.where((chosen > 0.5) & (kpos <= t_row), keep, NEG)
    s_d = _dot(ks_ref[0, pl.ds(q0, T), :], qt)
    first_w = [_weights(s_d[:, h], bias_d, bounded) for h in heads]
    l0 = jnp.concatenate([w[1] for w in first_w], axis=1)
    acc0 = _dot(vst_ref[0, :, pl.ds(q0, T)], jnp.concatenate([w[0].astype(BF16) for w in first_w], axis=1))

    before = jblk * SEL_BLK < q0
    sel_bias = jnp.where((sel > 0.5) & before, keep, NEG).astype(BF16)
    qx = jnp.concatenate([qt, jnp.concatenate([sel_bias] * R, axis=1)], axis=0)
    tk = min(ATT_KV_TILE, S)
    n_tiles = (q0 + tk - 1) // tk

    def masked_scores(kt):
        k0 = pl.multiple_of(kt * tk, tk)
        kx = jnp.concatenate([ks_ref[0, pl.ds(k0, tk), :], et_ref[pl.ds(k0, tk), :]], axis=1)
        return k0, _dot(kx, qx)

    if bounded:
        def kv_step(kt, carry):
            l, acc = carry
            k0, s = masked_scores(kt)
            p = [jnp.exp2(s[:, h]) for h in heads]
            l = l + jnp.concatenate([jnp.sum(p_r, axis=0, keepdims=True) for p_r in p], axis=1)
            pt = jnp.concatenate([p_r.astype(BF16) for p_r in p], axis=1)
            return l, acc + _dot(vst_ref[0, :, pl.ds(k0, tk)], pt)

        l_s, acc_s = lax.fori_loop(0, n_tiles, kv_step, (l0, acc0))
    else:
        def kv_step(kt, carry):
            m, l, acc = carry
            k0, s = masked_scores(kt)
            m_new, p = [], []
            for h in heads:
                m_r = jnp.maximum(m[:, h], jnp.max(s[:, h], axis=0, keepdims=True))
                p.append(jnp.exp2(s[:, h] - m_r))
                m_new.append(m_r)
            m_new = jnp.concatenate(m_new, axis=1)
            alpha = jnp.exp2(m - m_new)
            l_new = alpha * l + jnp.concatenate([jnp.sum(p_r, axis=0, keepdims=True) for p_r in p], axis=1)
            pt = jnp.concatenate([p_r.astype(BF16) for p_r in p], axis=1)
            return m_new, l_new, alpha * acc + _dot(vst_ref[0, :, pl.ds(k0, tk)], pt)

        m0 = jnp.concatenate([w[2] for w in first_w], axis=1)
        _, l_s, acc_s = lax.fori_loop(0, n_tiles, kv_step, (m0, l0, acc0))
    o_s = acc_s * (1.0 / jnp.maximum(l_s, TINY))

    gt = gate_ref[0].T
    out = []
    for r, h in enumerate(heads):
        c0 = N_BRANCH * r
        o_r = gt[c0:c0 + 1] * o_c[:, h] + gt[c0 + 1:c0 + 2] * o_s[:, h] + gt[c0 + 2:c0 + 3] * o_w[:, h]
        out.append(o_r.T)
    o_ref[0] = jnp.concatenate(out, axis=1).astype(o_ref.dtype)


def _nsa_attention(q, gates, cmp_kv, kvh, wselt, et, score_bound, B, G, R, S, dh, cmp_blk):
    T = _tile(S, ATT_Q_TILE)
    ns, ncp = wselt.shape
    vct = jnp.swapaxes(cmp_kv[1], -1, -2).reshape(B * G, dh, ncp)
    vst = jnp.swapaxes(kvh[3 * G:4 * G], -1, -2)
    vwt = jnp.swapaxes(kvh[5 * G:6 * G], -1, -2)
    shift = jnp.ceil(score_bound * 1.05) + 2.0

    def k_spec(br):
        return pl.BlockSpec((1, S, dh), lambda b, g, i: (br * G + g, b, 0))

    vt_spec = pl.BlockSpec((1, dh, S), lambda b, g, i: (g, 0, b))

    def run(bounded):
        return pl.pallas_call(
            functools.partial(_nsa_kernel, T=T, R=R, dh=dh, S=S, cmp_blk=cmp_blk, bounded=bounded),
            grid=(B, G, S // T),
            in_specs=[
                pl.BlockSpec(memory_space=pltpu.SMEM),
                pl.BlockSpec((1, T, R * dh), lambda b, g, i: (b, i, g)),
                pl.BlockSpec((1, T, LANES), lambda b, g, i: (b, i, g)),
                pl.BlockSpec((1, 1, 1, ncp, dh), lambda b, g, i: (0, b, g, 0, 0)),
                pl.BlockSpec((1, dh, ncp), lambda b, g, i: (b * G + g, 0, 0)),
                k_spec(2), vt_spec, k_spec(4), vt_spec,
                pl.BlockSpec((ns, ncp), lambda b, g, i: (0, 0)),
                pl.BlockSpec((S, ns), lambda b, g, i: (0, 0)),
            ],
            out_specs=pl.BlockSpec((1, T, R * dh), lambda b, g, i: (b, i, g)),
            out_shape=jax.ShapeDtypeStruct((B, S, G * R * dh), BF16),
            compiler_params=_params("parallel", "parallel", "arbitrary"),
            name="nsa_attention_bounded" if bounded else "nsa_attention",
        )(shift.reshape(1, 1), q, gates, cmp_kv, vct, kvh, vst, kvh, vwt, wselt, et)

    return lax.cond(shift <= MAX_SCORE_SHIFT, lambda: run(True), lambda: run(False))


def _selection_constants(S, ncp, cmp_blk):
    ns = S // SEL_BLK
    nc = (S - cmp_blk) // CMP_STRIDE + 1
    cs = jnp.arange(ncp) * CMP_STRIDE
    ss = jnp.arange(ns) * SEL_BLK
    ov = jnp.clip(jnp.minimum(cs[None, :] + cmp_blk, ss[:, None] + SEL_BLK) - jnp.maximum(cs[None, :], ss[:, None]),
                  0, None)
    wselt = jnp.where(jnp.arange(ncp)[None, :] < nc, ov / CMP_STRIDE, 0.0).astype(BF16)
    et = (jnp.arange(S)[:, None] // SEL_BLK == jnp.arange(ns)[None, :]).astype(BF16)
    return wselt, et


def kernel(x, c, w_ada, b_ada, ada_table, norm_mix_g, norm_ffn_g, w_ffn1, w_ffn2,
           conv_w_pw1, conv_b_pw1, conv_w_dw, conv_b_dw, conv_ln_g, conv_ln_b, conv_w_pw2, conv_b_pw2,
           kv_norm_g, w_kv, k_norm_g, cmp_pos, cmp_w1, cmp_b1, cmp_w2, cmp_b2,
           attn_w_qg, attn_q_norm_g, attn_w_o):
    B, S, D = x.shape
    M = B * S
    depth = ada_table.shape[0]
    n_a = conv_w_pw1.shape[0]
    dh = k_norm_g.shape[1]
    G = w_kv.shape[1] // (2 * N_BRANCH * dh)
    HD = attn_w_o.shape[1]
    R = HD // (G * dh)
    cmp_blk = cmp_pos.shape[1]
    assert R * N_BRANCH <= LANES and S % SEL_BLK == 0 and S % CMP_STRIDE == 0

    mod = _ada_mod(c, w_ada, b_ada, ada_table).reshape(depth, B, N_MOD, D)
    x2 = x.reshape(M, D)
    zeros_bd = jnp.zeros((B, D), F32)
    w_pw1, w_pw2 = conv_w_pw1.astype(BF16), conv_w_pw2.astype(BF16)
    w_qg, w_o = attn_w_qg.astype(BF16), attn_w_o.astype(BF16)
    w_ffn2_bf = w_ffn2.astype(BF16)
    kvh = cmp_kv = None
    wselt, et = _selection_constants(S, S // CMP_STRIDE, cmp_blk)

    for l in range(depth):
        sh_m, sc_m, g_m, sh_f, sc_f, g_f = (mod[l, :, i] for i in range(N_MOD))
        h = _norm_mod(x2.reshape(B, S, D), norm_mix_g[l], sc_m, sh_m)
        if l < n_a:
            u = _mm_glu(h.reshape(M, D), w_pw1, l, conv_b_pw1[l])
            v = _dwconv_ln_silu(u.reshape(B, S, D), conv_w_dw[l], conv_b_dw[l], conv_ln_g[l], conv_ln_b[l])
            x2 = _mm_resid(v.reshape(M, D), w_pw2, l, conv_b_pw2[l], x2, g_m, S)
        else:
            i = l - n_a
            hm = h.reshape(M, D)
            q = _mm_q(hm, w_qg, i, HD, attn_q_norm_g[i], dh)
            wg = attn_w_qg[i][:, HD:].reshape(D, G, R * N_BRANCH)
            wg = jnp.pad(wg, ((0, 0), (0, 0), (0, LANES - R * N_BRANCH))).reshape(D, G * LANES)
            gates = _mm_gate(hm, wg.astype(BF16))
            score_bound = (float(dh) ** 0.5 * LOG2E) * jnp.max(jnp.abs(attn_q_norm_g[i])) * jnp.max(jnp.abs(k_norm_g))
            o = _nsa_attention(q.reshape(B, S, HD), gates.reshape(B, S, G * LANES), cmp_kv, kvh, wselt, et,
                               score_bound, B, G, R, S, dh, cmp_blk)
            x2 = _mm_resid(o.reshape(M, HD), w_o, i, jnp.zeros((D,), F32), x2, g_m, S)
        h = _norm_mod(x2.reshape(B, S, D), norm_ffn_g[l], sc_f, sh_f)
        hid = _mm_relu2(h.reshape(M, D), w_ffn1, l)
        x2 = _mm_kacc_resid(hid, w_ffn2_bf, l, x2, g_f, S)
        if l == n_a - 1:
            hk = _norm_mod(x2.reshape(B, S, D), kv_norm_g, zeros_bd, zeros_bd)
            ones = jnp.ones((dh,), F32)
            gain = jnp.stack([ones, ones, k_norm_g[1], ones, k_norm_g[2], ones]).reshape(2 * N_BRANCH, 1, dh)
            flag = jnp.array([0.0, 0.0, 1.0, 0.0, 1.0, 0.0], F32)[:, None, None] * jnp.ones((1, 1, dh), F32)
            kvh = _mm_kv(hk.reshape(M, D), w_kv.astype(BF16), gain, flag, G, dh)
            cmp_kv = _compress(kvh, cmp_pos, cmp_w1, cmp_b1, cmp_w2, cmp_b2, k_norm_g[0], B, G, S, dh)
    return x2.reshape(B, S, D)
```
